```python
import jax, jax.numpy as jnp
from jax import lax
import numpy as np

D_MODEL = 1024
BATCH = 8
SEQ = 4096
DEPTH = 1

MIX_W = D_MODEL
ATT_HEADS = 8
QK_NOPE = 64
QK_ROPE = 32
V_HEAD = 64
ATT_W = ATT_HEADS * V_HEAD
Q_RANK = D_MODEL // 2
KV_RANK = D_MODEL // 4
ROPE_THETA = 10000.0
Q_BLOCK = 128
CONV_CH = MIX_W - ATT_W
CONV_GROUPS = 8
CONV_K = 31
IN_W = Q_RANK + KV_RANK + QK_ROPE + 2 * CONV_CH
PEER_HEADS = 8
PEER_NKEYS = 128
PEER_EXPERTS = PEER_NKEYS * PEER_NKEYS
PEER_TOPK = 16
PEER_DK = 128
PEER_DK_HALF = PEER_DK // 2
PEER_BLOCK = 128
PLE_DIM = 256
EPS = 1e-6

kernel_name = "hymba_mla_conformer_peer_layer"


def rmsnorm(x, g):
    xf = x.astype(jnp.float32)
    y = xf * lax.rsqrt(jnp.mean(xf * xf, axis=-1, keepdims=True) + EPS)
    return (y * g.astype(jnp.float32)).astype(x.dtype)


def layernorm(x, g, b):
    xf = x.astype(jnp.float32)
    mu = jnp.mean(xf, axis=-1, keepdims=True)
    var = jnp.mean(jnp.square(xf - mu), axis=-1, keepdims=True)
    y = (xf - mu) * lax.rsqrt(var + EPS)
    return (y * g.astype(jnp.float32) + b.astype(jnp.float32)).astype(x.dtype)


def rope(x, positions):
    half = x.shape[-1] // 2
    freqs = ROPE_THETA ** (-jnp.arange(half, dtype=jnp.float32) / half)
    ang = positions.astype(jnp.float32)[..., None] * freqs
    cos = jnp.cos(ang)[:, :, None, :]
    sin = jnp.sin(ang)[:, :, None, :]
    xf = x.astype(jnp.float32)
    x1, x2 = xf[..., :half], xf[..., half:]
    out = jnp.concatenate([x1 * cos - x2 * sin, x2 * cos + x1 * sin], axis=-1)
    return out.astype(x.dtype)


def causal_attention(q, k, v):
    B, S, H, Dk = q.shape
    Dv = v.shape[-1]
    nb = S // Q_BLOCK
    scale = Dk ** -0.5
    qb = q.reshape(B, nb, Q_BLOCK, H, Dk).transpose(1, 0, 2, 3, 4)
    k_pos = jnp.arange(S)

    def one_block(args):
        qi, bi = args
        s = jnp.einsum('bqhd,bkhd->bhqk', qi, k).astype(jnp.float32) * scale
        q_pos = bi * Q_BLOCK + jnp.arange(Q_BLOCK)
        mask = k_pos[None, :] <= q_pos[:, None]
        s = jnp.where(mask[None, None], s, jnp.finfo(jnp.float32).min)
        w = jax.nn.softmax(s, axis=-1).astype(v.dtype)
        return jnp.einsum('bhqk,bkhd->bqhd', w, v)

    out = lax.map(one_block, (qb, jnp.arange(nb)))
    return out.transpose(1, 0, 2, 3, 4).reshape(B, S, H * Dv)


def mla_group(q_lat, kv_lat, k_pe, positions, q_norm, w_uq, kv_norm, w_ukv):
    B, S, _ = q_lat.shape
    q = (rmsnorm(q_lat, q_norm) @ w_uq).reshape(B, S, ATT_HEADS, QK_NOPE + QK_ROPE)
    q_nope, q_pe = q[..., :QK_NOPE], q[..., QK_NOPE:]
    q_pe = rope(q_pe, positions)
    kv = (rmsnorm(kv_lat, kv_norm) @ w_ukv).reshape(B, S, ATT_HEADS, QK_NOPE + V_HEAD)
    k_nope, v = kv[..., :QK_NOPE], kv[..., QK_NOPE:]
    k_pe = rope(k_pe[:, :, None, :], positions)
    k_pe = jnp.broadcast_to(k_pe, (B, S, ATT_HEADS, QK_ROPE))
    q_full = jnp.concatenate([q_nope, q_pe], axis=-1)
    k_full = jnp.concatenate([k_nope, k_pe], axis=-1)
    return causal_attention(q_full, k_full, v)


def conformer_conv_group(u, conv_w, conv_b, ln_g, ln_b):
    a, gate = jnp.split(u, 2, axis=-1)
    y = a * jax.nn.sigmoid(gate)
    y = lax.conv_general_dilated(
        y, conv_w[:, None, :].astype(y.dtype), window_strides=(1,),
        padding=[(CONV_K - 1, 0)], dimension_numbers=('NWC', 'WIO', 'NWC'),
        feature_group_count=CONV_CH) + conv_b
    y = layernorm(y, ln_g, ln_b)
    return jax.nn.silu(y)


def peer(xn, w_q, sub_keys, u_tab, v_tab):
    B, S, D = xn.shape
    T = B * S
    xt = xn.reshape(T // PEER_BLOCK, PEER_BLOCK, D)

    def one_block(xb):
        P = xb.shape[0]
        q = (xb @ w_q).reshape(P, PEER_HEADS, 2, PEER_DK_HALF)
        s = jnp.einsum('phcd,hcnd->phcn', q, sub_keys).astype(jnp.float32)
        v1, i1 = lax.top_k(s[:, :, 0], PEER_TOPK)
        v2, i2 = lax.top_k(s[:, :, 1], PEER_TOPK)
        cand = (v1[..., :, None] + v2[..., None, :]).reshape(P, PEER_HEADS, PEER_TOPK * PEER_TOPK)
        cand_idx = (i1[..., :, None] * PEER_NKEYS + i2[..., None, :]).reshape(P, PEER_HEADS, PEER_TOPK * PEER_TOPK)
        best, pos = lax.top_k(cand, PEER_TOPK)
        e = jnp.take_along_axis(cand_idx, pos, axis=-1)
        g = jax.nn.softmax(best, axis=-1)
        u = u_tab[e]
        act = jax.nn.gelu(jnp.einsum('pd,phkd->phk', xb, u).astype(jnp.float32), approximate=False)
        coef = (g * act).astype(xb.dtype)
        return jnp.einsum('phk,phkd->pd', coef, v_tab[e])

    return lax.map(one_block, xt).reshape(B, S, D)


def setup_inputs(seed: int = 0) -> dict:
    key = jax.random.key(seed)
    ks = jax.random.split(key, 32)
    f = jnp.float32

    def nrm(k, shape, scale):
        return jax.random.normal(k, shape, f) * scale

    def gain(k, shape):
        return 1.0 + 0.02 * jax.random.normal(k, shape, f)

    L = DEPTH
    x = jax.random.normal(ks[0], (BATCH, SEQ, D_MODEL), f)
    p = jax.random.normal(ks[1], (DEPTH, BATCH, SEQ, PLE_DIM), f)
    offs = jax.random.randint(ks[2], (BATCH, 1), 0, 1024, dtype=jnp.int32)
    positions = offs + jnp.arange(SEQ, dtype=jnp.int32)[None, :]
    return {
        "x": x,
        "p": p,
        "positions": positions,
        "attn_norm": gain(ks[3], (L, D_MODEL)),
        "w_in": nrm(ks[4], (L, D_MODEL, IN_W), D_MODEL ** -0.5),
        "q_norm": gain(ks[5], (L, Q_RANK)),
        "w_uq": nrm(ks[6], (L, Q_RANK, ATT_HEADS * (QK_NOPE + QK_ROPE)), Q_RANK ** -0.5),
        "kv_norm": gain(ks[7], (L, KV_RANK)),
        "w_ukv": nrm(ks[8], (L, KV_RANK, ATT_HEADS * (QK_NOPE + V_HEAD)), KV_RANK ** -0.5),
        "conv_w": nrm(ks[9], (L, CONV_K, CONV_CH), CONV_K ** -0.5),
        "conv_b": nrm(ks[10], (L, CONV_CH), 0.02),
        "conv_ln_g": gain(ks[11], (L, CONV_CH)),
        "conv_ln_b": nrm(ks[12], (L, CONV_CH), 0.02),
        "attn_out_norm": gain(ks[13], (L, ATT_W)),
        "conv_out_norm": gain(ks[14], (L, CONV_CH)),
        "w_out": nrm(ks[15], (L, MIX_W, D_MODEL), MIX_W ** -0.5),
        "ffn_norm": gain(ks[16], (L, D_MODEL)),
        "peer_wq": nrm(ks[17], (L, D_MODEL, PEER_HEADS * PEER_DK), D_MODEL ** -0.5),
        "peer_keys": nrm(ks[18], (L, PEER_HEADS, 2, PEER_NKEYS, PEER_DK_HALF), PEER_DK_HALF ** -0.5),
        "peer_u": nrm(ks[19], (L, PEER_EXPERTS, D_MODEL), D_MODEL ** -0.5),
        "peer_v": nrm(ks[20], (L, PEER_EXPERTS, D_MODEL), (PEER_HEADS ** -0.5)),
        "pl_norm": gain(ks[21], (L, D_MODEL)),
        "pl_gate_w": nrm(ks[22], (L, D_MODEL, D_MODEL), D_MODEL ** -0.5),
        "pl_gate_b": nrm(ks[23], (L, D_MODEL), 0.02),
        "pl_proj": nrm(ks[24], (L, PLE_DIM, D_MODEL), PLE_DIM ** -0.5),
        "final_norm": gain(ks[25], (D_MODEL,)),
    }


def reference(x, p, positions, attn_norm, w_in, q_norm, w_uq, kv_norm, w_ukv,
              conv_w, conv_b, conv_ln_g, conv_ln_b, attn_out_norm, conv_out_norm,
              w_out, ffn_norm, peer_wq, peer_keys, peer_u, peer_v,
              pl_norm, pl_gate_w, pl_gate_b, pl_proj, final_norm):
    h = x
    split_pts = [Q_RANK, Q_RANK + KV_RANK, Q_RANK + KV_RANK + QK_ROPE]
    for i in range(DEPTH):
        hn = rmsnorm(h, attn_norm[i])
        z = hn @ w_in[i]
        q_lat, kv_lat, k_pe, conv_in = jnp.split(z, split_pts, axis=-1)
        att = mla_group(q_lat, kv_lat, k_pe, positions, q_norm[i], w_uq[i], kv_norm[i], w_ukv[i])
        cnv = conformer_conv_group(conv_in, conv_w[i], conv_b[i], conv_ln_g[i], conv_ln_b[i])
        mix = jnp.concatenate([rmsnorm(att, attn_out_norm[i]), rmsnorm(cnv, conv_out_norm[i])], axis=-1)
        h = h + mix @ w_out[i]
        h = h + peer(rmsnorm(h, ffn_norm[i]), peer_wq[i], peer_keys[i], peer_u[i], peer_v[i])
        gate = jax.nn.sigmoid(rmsnorm(h, pl_norm[i]) @ pl_gate_w[i] + pl_gate_b[i])
        h = h + gate * (p[i] @ pl_proj[i])
    return rmsnorm(h, final_norm)
```

```python
import functools
import math

import jax
import jax.numpy as jnp
import numpy as np
from jax import lax
from jax.experimental import pallas as pl
from jax.experimental.pallas import tpu as pltpu

F32 = jnp.float32
BF16 = jnp.bfloat16

D_MODEL_ = 1024
N_HEADS = 8
QK_NOPE_ = 64
QK_ROPE_ = 32
ROPE_HALF = QK_ROPE_ // 2
V_HEAD_ = 64
HEAD_PAD = 128
Q_RANK_ = 512
KV_RANK_ = 256
CONV_CH_ = 512
CONV_K_ = 31
CONV_HALO = 32
ROPE_THETA_ = 10000.0
PEER_HEADS_ = 8
PEER_NKEYS_ = 128
PEER_TOPK_ = 16
PEER_DK_HALF_ = 64
PLE_DIM_ = 256
EPS_ = 1e-6
NEG_BIG = -1e30
LANES = 128
VMEM_LIMIT = 56 * 1024 * 1024

NT_DIMS = (((1,), (1,)), ((), ()))


def _rms(x, g):
    return x * lax.rsqrt(jnp.mean(x * x, axis=-1, keepdims=True) + EPS_) * g


def _dot(a, b):
    return jnp.dot(a, b, preferred_element_type=F32)


def _inproj_kernel(x_ref, pos_ref, freq_ref, g_attn_ref, w_in_ref, g_q_ref, w_uq_ref,
                   g_kv_ref, w_ukv_ref, perm_ref, q_ref, k_ref, v_ref, glu_ref):
    x = x_ref[...]
    xn = _rms(x, g_attn_ref[...]).astype(BF16)
    z = _dot(xn, w_in_ref[...])
    q_lat = z[:, 0:512]
    kv_lat = z[:, 512:768]
    k1 = z[:, 768:896]
    k2 = z[:, 896:1024]
    glu_ref[...] = z[:, 1024:1536] * jax.nn.sigmoid(z[:, 1536:2048])

    q = _dot(_rms(q_lat, g_q_ref[...]).astype(BF16), w_uq_ref[...])
    kv = _dot(_rms(kv_lat, g_kv_ref[...]).astype(BF16), w_ukv_ref[...])
    ang = pos_ref[...] * freq_ref[...]
    cos = jnp.cos(ang)
    sin = jnp.sin(ang)
    p1 = q[:, 512:640]
    p2 = q[:, 640:768]
    qcat = jnp.concatenate([q[:, 0:512], p1 * cos - p2 * sin, p2 * cos + p1 * sin], axis=-1)
    kcat = jnp.concatenate([kv[:, 0:512], k1 * cos - k2 * sin, k2 * cos + k1 * sin], axis=-1)
    q_ref[...] = _dot(qcat.astype(BF16), perm_ref[...]).astype(BF16)
    k_ref[...] = _dot(kcat.astype(BF16), perm_ref[...]).astype(BF16)
    v_ref[...] = kv[:, 512:1024].astype(BF16)


def _inproj(x2, posf, freq, g_attn, w_in_p, g_q, w_uq_p, g_kv, w_ukv_p, perm, tm):
    t = x2.shape[0]
    full = lambda shape: pl.BlockSpec(shape, lambda i: (0,) * len(shape))
    return pl.pallas_call(
        _inproj_kernel,
        grid=(t // tm,),
        in_specs=[
            pl.BlockSpec((tm, D_MODEL_), lambda i: (i, 0)),
            pl.BlockSpec((tm, 1), lambda i: (i, 0)),
            full((1, LANES)), full((1, D_MODEL_)), full(w_in_p.shape), full((1, Q_RANK_)),
            full(w_uq_p.shape), full((1, KV_RANK_)), full(w_ukv_p.shape), full(perm.shape),
        ],
        out_specs=[
            pl.BlockSpec((tm, N_HEADS * HEAD_PAD), lambda i: (i, 0)),
            pl.BlockSpec((tm, N_HEADS * HEAD_PAD), lambda i: (i, 0)),
            pl.BlockSpec((tm, N_HEADS * V_HEAD_), lambda i: (i, 0)),
            pl.BlockSpec((tm, CONV_CH_), lambda i: (i, 0)),
        ],
        out_shape=[
            jax.ShapeDtypeStruct((t, N_HEADS * HEAD_PAD), BF16),
            jax.ShapeDtypeStruct((t, N_HEADS * HEAD_PAD), BF16),
            jax.ShapeDtypeStruct((t, N_HEADS * V_HEAD_), BF16),
            jax.ShapeDtypeStruct((t, CONV_CH_), F32),
        ],
        compiler_params=pltpu.CompilerParams(
            dimension_semantics=("parallel",), vmem_limit_bytes=VMEM_LIMIT),
        name="inproj",
    )(x2, posf, freq, g_attn, w_in_p, g_q, w_uq_p, g_kv, w_ukv_p, perm)


def _attn_kernel(q_ref, k_ref, v_ref, o_ref, m_sc, l_sc, acc_sc, *, tq, tk, scale):
    qi = pl.program_id(2)
    ki = pl.program_id(3)

    @pl.when(ki == 0)
    def _init():
        m_sc[...] = jnp.full(m_sc.shape, NEG_BIG, F32)
        l_sc[...] = jnp.zeros(l_sc.shape, F32)
        acc_sc[...] = jnp.zeros(acc_sc.shape, F32)

    @pl.when(ki <= qi)
    def _step():
        row = qi * tq + lax.broadcasted_iota(jnp.int32, (tq, tk), 0)
        col = ki * tk + lax.broadcasted_iota(jnp.int32, (tq, tk), 1)
        keep = col <= row
        v = v_ref[...]
        for hh in range(2):
            q = q_ref[:, hh * HEAD_PAD:(hh + 1) * HEAD_PAD]
            k = k_ref[:, hh * HEAD_PAD:(hh + 1) * HEAD_PAD]
            s = lax.dot_general(q, k, NT_DIMS, preferred_element_type=F32) * scale
            s = jnp.where(keep, s, NEG_BIG)
            m_prev = m_sc[hh]
            m_new = jnp.maximum(m_prev, jnp.max(s, axis=-1, keepdims=True))
            alpha = jnp.exp(m_prev - m_new)
            p = jnp.exp(s - m_new)
            l_sc[hh] = alpha * l_sc[hh] + jnp.sum(p, axis=-1, keepdims=True)
            acc_sc[hh] = alpha * acc_sc[hh] + _dot(p.astype(BF16), v)
            m_sc[hh] = m_new

    @pl.when(ki == pl.num_programs(3) - 1)
    def _fin():
        o0 = acc_sc[0] / l_sc[0]
        o1 = acc_sc[1] / l_sc[1]
        lane = lax.broadcasted_iota(jnp.int32, o0.shape, 1)
        o_ref[...] = jnp.where(lane < V_HEAD_, o0, o1)


def _attention(q, k, v, tq):
    b, s, _ = q.shape
    nq = s // tq
    scale = float((QK_NOPE_ + QK_ROPE_) ** -0.5)
    return pl.pallas_call(
        functools.partial(_attn_kernel, tq=tq, tk=tq, scale=scale),
        grid=(b, N_HEADS // 2, nq, nq),
        in_specs=[
            pl.BlockSpec((None, tq, 2 * HEAD_PAD), lambda b_, h, qi, ki: (b_, qi, h)),
            pl.BlockSpec((None, tq, 2 * HEAD_PAD), lambda b_, h, qi, ki: (b_, jnp.minimum(ki, qi), h)),
            pl.BlockSpec((None, tq, 2 * V_HEAD_), lambda b_, h, qi, ki: (b_, jnp.minimum(ki, qi), h)),
        ],
        out_specs=pl.BlockSpec((None, tq, 2 * V_HEAD_), lambda b_, h, qi, ki: (b_, qi, h)),
        out_shape=jax.ShapeDtypeStruct((b, s, N_HEADS * V_HEAD_), F32),
        scratch_shapes=[
            pltpu.VMEM((2, tq, 1), F32),
            pltpu.VMEM((2, tq, 1), F32),
            pltpu.VMEM((2, tq, 2 * V_HEAD_), F32),
        ],
        compiler_params=pltpu.CompilerParams(
            dimension_semantics=("parallel", "parallel", "parallel", "arbitrary"),
            vmem_limit_bytes=VMEM_LIMIT),
        name="attn",
    )(q, k, v)


def _conv_kernel(y_ref, halo_ref, w_ref, b_ref, lng_ref, lnb_ref, g_out_ref, o_ref, win_sc, *, ts, rc):
    i = pl.program_id(1)
    halo = halo_ref[...]
    win_sc[0:CONV_HALO, :] = jnp.where(i == 0, jnp.zeros_like(halo), halo)
    win_sc[CONV_HALO:CONV_HALO + ts, :] = y_ref[...]
    first = CONV_HALO - (CONV_K_ - 1)
    bias = b_ref[...]
    for r0 in range(0, ts, rc):
        acc = jnp.broadcast_to(bias, (rc, CONV_CH_))
        for kk in range(CONV_K_):
            acc = acc + w_ref[kk:kk + 1, :] * win_sc[r0 + first + kk:r0 + first + kk + rc, :]
        mu = jnp.mean(acc, axis=-1, keepdims=True)
        cen = acc - mu
        var = jnp.mean(cen * cen, axis=-1, keepdims=True)
        yn = cen * lax.rsqrt(var + EPS_) * lng_ref[...] + lnb_ref[...]
        act = yn * jax.nn.sigmoid(yn)
        o_ref[r0:r0 + rc, :] = _rms(act, g_out_ref[...]).astype(BF16)


def _conv_branch(glu, conv_w, conv_b, ln_g, ln_b, g_out, ts):
    b, s, _ = glu.shape
    rc = 32
    full = lambda shape: pl.BlockSpec(shape, lambda b_, i: (0,) * len(shape))
    per_halo = ts // CONV_HALO
    return pl.pallas_call(
        functools.partial(_conv_kernel, ts=ts, rc=rc),
        grid=(b, s // ts),
        in_specs=[
            pl.BlockSpec((None, ts, CONV_CH_), lambda b_, i: (b_, i, 0)),
            pl.BlockSpec((None, CONV_HALO, CONV_CH_),
                         lambda b_, i: (b_, jnp.maximum(i * per_halo - 1, 0), 0)),
            full((CONV_K_, CONV_CH_)), full((1, CONV_CH_)), full((1, CONV_CH_)),
            full((1, CONV_CH_)), full((1, CONV_CH_)),
        ],
        out_specs=pl.BlockSpec((None, ts, CONV_CH_), lambda b_, i: (b_, i, 0)),
        out_shape=jax.ShapeDtypeStruct((b, s, CONV_CH_), BF16),
        scratch_shapes=[pltpu.VMEM((CONV_HALO + ts, CONV_CH_), F32)],
        compiler_params=pltpu.CompilerParams(
            dimension_semantics=("parallel", "parallel"), vmem_limit_bytes=VMEM_LIMIT),
        name="conv",
    )(glu, glu, conv_w, conv_b, ln_g, ln_b, g_out)


def _outproj_kernel(x_ref, att_ref, cnv_ref, g_att_ref, w_att_ref, w_cnv_ref, g_ffn_ref,
                    h_ref, hn_ref):
    att_n = _rms(att_ref[...], g_att_ref[...]).astype(BF16)
    h = x_ref[...] + _dot(att_n, w_att_ref[...]) + _dot(cnv_ref[...], w_cnv_ref[...])
    h_ref[...] = h
    hn_ref[...] = _rms(h, g_ffn_ref[...]).astype(BF16)


def _outproj(x2, att, cnv, g_att, w_att, w_cnv, g_ffn, tm):
    t = x2.shape[0]
    full = lambda shape: pl.BlockSpec(shape, lambda i: (0,) * len(shape))
    return pl.pallas_call(
        _outproj_kernel,
        grid=(t // tm,),
        in_specs=[
            pl.BlockSpec((tm, D_MODEL_), lambda i: (i, 0)),
            pl.BlockSpec((tm, N_HEADS * V_HEAD_), lambda i: (i, 0)),
            pl.BlockSpec((tm, CONV_CH_), lambda i: (i, 0)),
            full((1, N_HEADS * V_HEAD_)), full(w_att.shape), full(w_cnv.shape), full((1, D_MODEL_)),
        ],
        out_specs=[
            pl.BlockSpec((tm, D_MODEL_), lambda i: (i, 0)),
            pl.BlockSpec((tm, D_MODEL_), lambda i: (i, 0)),
        ],
        out_shape=[
            jax.ShapeDtypeStruct((t, D_MODEL_), F32),
            jax.ShapeDtypeStruct((t, D_MODEL_), BF16),
        ],
        compiler_params=pltpu.CompilerParams(
            dimension_semantics=("parallel",), vmem_limit_bytes=VMEM_LIMIT),
        name="outproj",
    )(x2, att, cnv, g_att, w_att, w_cnv, g_ffn)


N_TOP = PEER_TOPK_ + 1
TOP_ROWS = 24


def _extract_top(vals, n):
    out = []
    for _ in range(n):
        m = jnp.max(vals, axis=0, keepdims=True)
        out.append(m)
        vals = jnp.where(vals == m, -jnp.inf, vals)
    return out


def _peer_kernel(h_ref, xn_ref, wqT_ref, keys_ref, u_ref, vT_ref, o_ref,
                 s_sc, vtop_sc, e2_sc, thr_sc, a1_sc, at_sc, c_sc, outT_sc, *, tb, ec):
    j = pl.program_id(1)
    ncol = tb // LANES
    n_i1 = ec // PEER_NKEYS_

    @pl.when(j == 0)
    def _scores_and_thresholds():
        qT = lax.dot_general(wqT_ref[...], xn_ref[...], NT_DIMS,
                             preferred_element_type=F32).astype(BF16)
        for hc in range(2 * PEER_HEADS_):
            s_sc[hc] = _dot(keys_ref[hc], qT[hc * PEER_DK_HALF_:(hc + 1) * PEER_DK_HALF_, :])
        vtop_sc[...] = jnp.full(vtop_sc.shape, -jnp.inf, F32)

        def top_body(idx, carry):
            hc = idx // ncol
            cols = pl.ds(pl.multiple_of((idx % ncol) * LANES, LANES), LANES)
            tops = _extract_top(s_sc[hc, :, cols], N_TOP)
            for r, m in enumerate(tops):
                vtop_sc[hc, r:r + 1, cols] = m
            return carry

        lax.fori_loop(0, 2 * PEER_HEADS_ * ncol, top_body, 0)

        def thr_body(idx, carry):
            h = idx // ncol
            cols = pl.ds(pl.multiple_of((idx % ncol) * LANES, LANES), LANES)
            v1 = vtop_sc[2 * h, :, cols]
            v2 = vtop_sc[2 * h + 1, :, cols]
            blocks = [v1[0:1] + v2]
            blocks += [v1[a:a + 1] + v2[0:8] for a in range(1, 8)]
            blocks += [v1[8:TOP_ROWS] + v2[0:1]]
            cs = _extract_top(jnp.concatenate(blocks, axis=0), N_TOP)
            tau = 0.5 * (cs[PEER_TOPK_ - 1] + cs[PEER_TOPK_])
            z = jnp.exp(cs[0] - cs[0])
            for r in range(1, PEER_TOPK_):
                z = z + jnp.exp(cs[r] - cs[0])
            s1 = s_sc[2 * h, :, cols]
            s2 = s_sc[2 * h + 1, :, cols]
            e2_sc[h, :, cols] = jnp.exp(s2 - v2[0:1])
            n_chunk = PEER_NKEYS_ // n_i1
            thr_sc[:, h, :, cols] = (tau - s1).reshape(n_chunk, n_i1, LANES)
            a1_sc[:, h, :, cols] = (jnp.exp(s1 - v1[0:1]) / z).reshape(n_chunk, n_i1, LANES)
            return carry

        lax.fori_loop(0, PEER_HEADS_ * ncol, thr_body, 0)
        outT_sc[...] = jnp.zeros(outT_sc.shape, F32)

    at_sc[...] = lax.dot_general(u_ref[...], xn_ref[...], NT_DIMS, preferred_element_type=F32)

    def w_body(col, carry):
        cols = pl.ds(pl.multiple_of(col * LANES, LANES), LANES)
        for i1l in range(n_i1):
            rows = slice(i1l * PEER_NKEYS_, (i1l + 1) * PEER_NKEYS_)
            w = jnp.zeros((PEER_NKEYS_, LANES), F32)
            for h in range(PEER_HEADS_):
                thr = thr_sc[j, h, i1l:i1l + 1, cols]
                a1 = a1_sc[j, h, i1l:i1l + 1, cols]
                s2 = s_sc[2 * h + 1, :, cols]
                w = w + jnp.where(s2 >= thr, e2_sc[h, :, cols], 0.0) * a1
            a = at_sc[rows, cols]
            gelu = 0.5 * a * (1.0 + lax.erf(a * np.float32(math.sqrt(0.5))))
            c_sc[rows, cols] = (gelu * w).astype(BF16)
        return carry

    lax.fori_loop(0, ncol, w_body, 0)
    outT_sc[...] += _dot(vT_ref[...], c_sc[...])

    @pl.when(j == pl.num_programs(1) - 1)
    def _fin():
        o_ref[...] = h_ref[...] + outT_sc[...].T


def _peer(h1, hn, wqT, keys, u_tab, vT_tab, tb, ec):
    t = h1.shape[0]
    n_exp = u_tab.shape[0]
    return pl.pallas_call(
        functools.partial(_peer_kernel, tb=tb, ec=ec),
        grid=(t // tb, n_exp // ec),
        in_specs=[
            pl.BlockSpec((tb, D_MODEL_), lambda i, j: (i, 0)),
            pl.BlockSpec((tb, D_MODEL_), lambda i, j: (i, 0)),
            pl.BlockSpec(wqT.shape, lambda i, j: (0, 0)),
            pl.BlockSpec(keys.shape, lambda i, j: (0, 0, 0)),
            pl.BlockSpec((ec, D_MODEL_), lambda i, j: (j, 0)),
            pl.BlockSpec((D_MODEL_, ec), lambda i, j: (0, j)),
        ],
        out_specs=pl.BlockSpec((tb, D_MODEL_), lambda i, j: (i, 0)),
        out_shape=jax.ShapeDtypeStruct((t, D_MODEL_), F32),
        scratch_shapes=[
            pltpu.VMEM((2 * PEER_HEADS_, PEER_NKEYS_, tb), F32),
            pltpu.VMEM((2 * PEER_HEADS_, TOP_ROWS, tb), F32),
            pltpu.VMEM((PEER_HEADS_, PEER_NKEYS_, tb), F32),
            pltpu.VMEM((n_exp // ec, PEER_HEADS_, ec // PEER_NKEYS_, tb), F32),
            pltpu.VMEM((n_exp // ec, PEER_HEADS_, ec // PEER_NKEYS_, tb), F32),
            pltpu.VMEM((ec, tb), F32),
            pltpu.VMEM((ec, tb), BF16),
            pltpu.VMEM((D_MODEL_, tb), F32),
        ],
        compiler_params=pltpu.CompilerParams(
            dimension_semantics=("parallel", "arbitrary"), vmem_limit_bytes=VMEM_LIMIT),
        name="peer",
    )(h1, hn, wqT, keys, u_tab, vT_tab)


def _ple_kernel(h_ref, p_ref, g_pl_ref, wg_ref, bg_ref, wp_ref, g_fin_ref, o_ref, *, last_layer):
    h = h_ref[...]
    gate = jax.nn.sigmoid(_dot(_rms(h, g_pl_ref[...]).astype(BF16), wg_ref[...]) + bg_ref[...])
    h = h + gate * _dot(p_ref[...].astype(BF16), wp_ref[...])
    o_ref[...] = _rms(h, g_fin_ref[...]) if last_layer else h


def _ple(h2, p2, g_pl, wg, bg, wp, g_fin, last_layer, tm):
    t = h2.shape[0]
    full = lambda shape: pl.BlockSpec(shape, lambda i: (0,) * len(shape))
    return pl.pallas_call(
        functools.partial(_ple_kernel, last_layer=last_layer),
        grid=(t // tm,),
        in_specs=[
            pl.BlockSpec((tm, D_MODEL_), lambda i: (i, 0)),
            pl.BlockSpec((tm, PLE_DIM_), lambda i: (i, 0)),
            full((1, D_MODEL_)), full(wg.shape), full((1, D_MODEL_)), full(wp.shape), full((1, D_MODEL_)),
        ],
        out_specs=pl.BlockSpec((tm, D_MODEL_), lambda i: (i, 0)),
        out_shape=jax.ShapeDtypeStruct((t, D_MODEL_), F32),
        compiler_params=pltpu.CompilerParams(
            dimension_semantics=("parallel",), vmem_limit_bytes=VMEM_LIMIT),
        name="ple",
    )(h2, p2, g_pl, wg, bg, wp, g_fin)


def _head_perm():
    pm = np.zeros((N_HEADS * (QK_NOPE_ + QK_ROPE_), N_HEADS * HEAD_PAD), np.float32)
    for h in range(N_HEADS):
        for d in range(QK_NOPE_):
            pm[h * QK_NOPE_ + d, h * HEAD_PAD + d] = 1.0
        for d in range(ROPE_HALF):
            pm[N_HEADS * QK_NOPE_ + h * ROPE_HALF + d, h * HEAD_PAD + QK_NOPE_ + d] = 1.0
            pm[N_HEADS * (QK_NOPE_ + ROPE_HALF) + h * ROPE_HALF + d,
               h * HEAD_PAD + QK_NOPE_ + ROPE_HALF + d] = 1.0
    return jnp.asarray(pm, BF16)


def _prep_w_in(w_in):
    q_lat = w_in[:, 0:Q_RANK_]
    kv_lat = w_in[:, Q_RANK_:Q_RANK_ + KV_RANK_]
    o = Q_RANK_ + KV_RANK_
    k1 = jnp.tile(w_in[:, o:o + ROPE_HALF], (1, N_HEADS))
    k2 = jnp.tile(w_in[:, o + ROPE_HALF:o + QK_ROPE_], (1, N_HEADS))
    conv = w_in[:, o + QK_ROPE_:]
    return jnp.concatenate([q_lat, kv_lat, k1, k2, conv], axis=1).astype(BF16)


def _prep_w_uq(w_uq):
    w = w_uq.reshape(Q_RANK_, N_HEADS, QK_NOPE_ + QK_ROPE_)
    nope = w[:, :, :QK_NOPE_].reshape(Q_RANK_, -1)
    r1 = w[:, :, QK_NOPE_:QK_NOPE_ + ROPE_HALF].reshape(Q_RANK_, -1)
    r2 = w[:, :, QK_NOPE_ + ROPE_HALF:].reshape(Q_RANK_, -1)
    return jnp.concatenate([nope, r1, r2], axis=1).astype(BF16)


def _prep_w_ukv(w_ukv):
    w = w_ukv.reshape(KV_RANK_, N_HEADS, QK_NOPE_ + V_HEAD_)
    k_nope = w[:, :, :QK_NOPE_].reshape(KV_RANK_, -1)
    v = w[:, :, QK_NOPE_:].reshape(KV_RANK_, -1)
    return jnp.concatenate([k_nope, v], axis=1).astype(BF16)


def _tile(n, want):
    return want if n % want == 0 else n


def kernel(x, p, positions, attn_norm, w_in, q_norm, w_uq, kv_norm, w_ukv, conv_w, conv_b, conv_ln_g, conv_ln_b, attn_out_norm, conv_out_norm, w_out, ffn_norm, peer_wq, peer_keys, peer_u, peer_v, pl_norm, pl_gate_w, pl_gate_b, pl_proj, final_norm):
    b, s, d = x.shape
    t = b * s
    depth = w_in.shape[0]
    tm = _tile(t, 512)
    tseq = _tile(s, 512)
    row = lambda a: a.reshape(1, -1)

    h = x.reshape(t, d)
    posf = positions.astype(F32).reshape(t, 1)
    half_freqs = ROPE_THETA_ ** (-jnp.arange(ROPE_HALF, dtype=F32) / ROPE_HALF)
    freq = jnp.tile(half_freqs, N_HEADS).reshape(1, LANES)
    perm = _head_perm()

    for i in range(depth):
        q, k, v, glu = _inproj(h, posf, freq, row(attn_norm[i]), _prep_w_in(w_in[i]), row(q_norm[i]),
                               _prep_w_uq(w_uq[i]), row(kv_norm[i]), _prep_w_ukv(w_ukv[i]), perm, tm)
        att = _attention(q.reshape(b, s, -1), k.reshape(b, s, -1), v.reshape(b, s, -1), tseq)
        cnv = _conv_branch(glu.reshape(b, s, -1), conv_w[i], row(conv_b[i]), row(conv_ln_g[i]),
                           row(conv_ln_b[i]), row(conv_out_norm[i]), tseq)
        w_o = w_out[i].astype(BF16)
        h1, hn = _outproj(h, att.reshape(t, -1), cnv.reshape(t, -1), row(attn_out_norm[i]),
                          w_o[:N_HEADS * V_HEAD_], w_o[N_HEADS * V_HEAD_:], row(ffn_norm[i]), tm)
        keys = peer_keys[i].reshape(2 * PEER_HEADS_, PEER_NKEYS_, PEER_DK_HALF_).astype(BF16)
        h2 = _peer(h1, hn, peer_wq[i].T.astype(BF16), keys, peer_u[i].astype(BF16),
                   peer_v[i].T.astype(BF16), tm, 1024)
        h = _ple(h2, p[i].reshape(t, -1), row(pl_norm[i]), pl_gate_w[i].astype(BF16),
                 row(pl_gate_b[i]), pl_proj[i].astype(BF16), row(final_norm), i == depth - 1, tm)
    return h.reshape(b, s, d)
```

```python
import functools
import math

import jax
import jax.numpy as jnp
import numpy as np
from jax import lax
from jax.experimental import pallas as pl
from jax.experimental.pallas import tpu as pltpu

F32 = jnp.float32
BF16 = jnp.bfloat16

D_MODEL_ = 1024
N_HEADS = 8
QK_NOPE_ = 64
QK_ROPE_ = 32
ROPE_HALF = QK_ROPE_ // 2
V_HEAD_ = 64
HEAD_PAD = 128
Q_RANK_ = 512
KV_RANK_ = 256
CONV_CH_ = 512
CONV_K_ = 31
CONV_HALO = 32
ROPE_THETA_ = 10000.0
PEER_HEADS_ = 8
PEER_NKEYS_ = 128
PEER_TOPK_ = 16
PEER_DK_HALF_ = 64
PLE_DIM_ = 256
EPS_ = 1e-6
NEG_BIG = -1e30
LANES = 128
VMEM_LIMIT = 56 * 1024 * 1024

NT_DIMS = (((1,), (1,)), ((), ()))
Q_PRESCALE = (QK_NOPE_ + QK_ROPE_) ** -0.5 * math.log2(math.e)


def _rms(x, g):
    return x * lax.rsqrt(jnp.mean(x * x, axis=-1, keepdims=True) + EPS_) * g


def _dot(a, b):
    return jnp.dot(a, b, preferred_element_type=F32)


def _inproj_kernel(x_ref, pos_ref, freq_ref, g_attn_ref, w_in_ref, g_q_ref, w_uq_ref,
                   g_kv_ref, w_ukv_ref, perm_ref, q_ref, k_ref, v_ref, glu_ref):
    x = x_ref[...]
    xn = _rms(x, g_attn_ref[...]).astype(BF16)
    z = _dot(xn, w_in_ref[...])
    q_lat = z[:, 0:512]
    kv_lat = z[:, 512:768]
    k1 = z[:, 768:896]
    k2 = z[:, 896:1024]
    glu_ref[...] = z[:, 1024:1536] * jax.nn.sigmoid(z[:, 1536:2048])

    q = _dot(_rms(q_lat, g_q_ref[...]).astype(BF16), w_uq_ref[...])
    kv = _dot(_rms(kv_lat, g_kv_ref[...]).astype(BF16), w_ukv_ref[...])
    ang = pos_ref[...] * freq_ref[...]
    cos = jnp.cos(ang)
    sin = jnp.sin(ang)
    p1 = q[:, 512:640]
    p2 = q[:, 640:768]
    qcat = jnp.concatenate([q[:, 0:512], p1 * cos - p2 * sin, p2 * cos + p1 * sin], axis=-1)
    qcat = qcat * np.float32(Q_PRESCALE)
    kcat = jnp.concatenate([kv[:, 0:512], k1 * cos - k2 * sin, k2 * cos + k1 * sin], axis=-1)
    q_ref[...] = _dot(qcat.astype(BF16), perm_ref[...]).astype(BF16)
    k_ref[...] = _dot(kcat.astype(BF16), perm_ref[...]).astype(BF16)
    v_ref[...] = kv[:, 512:1024].astype(BF16)


def _inproj(x2, posf, freq, g_attn, w_in_p, g_q, w_uq_p, g_kv, w_ukv_p, perm, tm):
    t = x2.shape[0]
    full = lambda shape: pl.BlockSpec(shape, lambda i: (0,) * len(shape))
    return pl.pallas_call(
        _inproj_kernel,
        grid=(t // tm,),
        in_specs=[
            pl.BlockSpec((tm, D_MODEL_), lambda i: (i, 0)),
            pl.BlockSpec((tm, 1), lambda i: (i, 0)),
            full((1, LANES)), full((1, D_MODEL_)), full(w_in_p.shape), full((1, Q_RANK_)),
            full(w_uq_p.shape), full((1, KV_RANK_)), full(w_ukv_p.shape), full(perm.shape),
        ],
        out_specs=[
            pl.BlockSpec((tm, N_HEADS * HEAD_PAD), lambda i: (i, 0)),
            pl.BlockSpec((tm, N_HEADS * HEAD_PAD), lambda i: (i, 0)),
            pl.BlockSpec((tm, N_HEADS * V_HEAD_), lambda i: (i, 0)),
            pl.BlockSpec((tm, CONV_CH_), lambda i: (i, 0)),
        ],
        out_shape=[
            jax.ShapeDtypeStruct((t, N_HEADS * HEAD_PAD), BF16),
            jax.ShapeDtypeStruct((t, N_HEADS * HEAD_PAD), BF16),
            jax.ShapeDtypeStruct((t, N_HEADS * V_HEAD_), BF16),
            jax.ShapeDtypeStruct((t, CONV_CH_), F32),
        ],
        compiler_params=pltpu.CompilerParams(
            dimension_semantics=("parallel",), vmem_limit_bytes=VMEM_LIMIT),
        name="inproj",
    )(x2, posf, freq, g_attn, w_in_p, g_q, w_uq_p, g_kv, w_ukv_p, perm)


def _attn_kernel(qi_ref, ki_ref, q_ref, k_ref, v_ref, o_ref, m_sc, l_sc, acc_sc, *, tq):
    t = pl.program_id(2)
    qi = qi_ref[t]
    ki = ki_ref[t]

    @pl.when(ki == 0)
    def _init():
        m_sc[...] = jnp.full(m_sc.shape, NEG_BIG, F32)
        l_sc[...] = jnp.zeros(l_sc.shape, F32)
        acc_sc[...] = jnp.zeros(acc_sc.shape, F32)

    def step(diagonal):
        v = v_ref[...]
        for hh in range(2):
            q = q_ref[:, hh * HEAD_PAD:(hh + 1) * HEAD_PAD]
            k = k_ref[:, hh * HEAD_PAD:(hh + 1) * HEAD_PAD]
            s = lax.dot_general(q, k, NT_DIMS, preferred_element_type=F32)
            if diagonal:
                row = lax.broadcasted_iota(jnp.int32, (tq, tq), 0)
                col = lax.broadcasted_iota(jnp.int32, (tq, tq), 1)
                s = jnp.where(col <= row, s, NEG_BIG)
            m_prev = m_sc[hh]
            m_new = jnp.maximum(m_prev, jnp.max(s, axis=-1, keepdims=True))
            alpha = jnp.exp2(m_prev - m_new)
            p = jnp.exp2(s - m_new)
            psum = p[:, 0:LANES]
            for c in range(1, tq // LANES):
                psum = psum + p[:, c * LANES:(c + 1) * LANES]
            l_sc[hh] = alpha * l_sc[hh] + psum
            acc_sc[hh] = alpha * acc_sc[hh] + _dot(p.astype(BF16), v)
            m_sc[hh] = m_new

    @pl.when(ki < qi)
    def _full_tile():
        step(False)

    @pl.when(ki == qi)
    def _diagonal_tile():
        step(True)
        o0 = acc_sc[0] / jnp.sum(l_sc[0], axis=-1, keepdims=True)
        o1 = acc_sc[1] / jnp.sum(l_sc[1], axis=-1, keepdims=True)
        lane = lax.broadcasted_iota(jnp.int32, o0.shape, 1)
        o_ref[...] = jnp.where(lane < V_HEAD_, o0, o1)


def _attention(q, k, v, tq):
    b, s, _ = q.shape
    nq = s // tq
    qi_tab = np.asarray([qi for qi in range(nq) for _ in range(qi + 1)], np.int32)
    ki_tab = np.asarray([ki for qi in range(nq) for ki in range(qi + 1)], np.int32)
    grid_spec = pltpu.PrefetchScalarGridSpec(
        num_scalar_prefetch=2,
        grid=(b, N_HEADS // 2, len(qi_tab)),
        in_specs=[
            pl.BlockSpec((None, tq, 2 * HEAD_PAD), lambda b_, h, t, qt, kt: (b_, qt[t], h)),
            pl.BlockSpec((None, tq, 2 * HEAD_PAD), lambda b_, h, t, qt, kt: (b_, kt[t], h)),
            pl.BlockSpec((None, tq, 2 * V_HEAD_), lambda b_, h, t, qt, kt: (b_, kt[t], h)),
        ],
        out_specs=pl.BlockSpec((None, tq, 2 * V_HEAD_), lambda b_, h, t, qt, kt: (b_, qt[t], h)),
        scratch_shapes=[
            pltpu.VMEM((2, tq, 1), F32),
            pltpu.VMEM((2, tq, LANES), F32),
            pltpu.VMEM((2, tq, 2 * V_HEAD_), F32),
        ],
    )
    return pl.pallas_call(
        functools.partial(_attn_kernel, tq=tq),
        grid_spec=grid_spec,
        out_shape=jax.ShapeDtypeStruct((b, s, N_HEADS * V_HEAD_), F32),
        compiler_params=pltpu.CompilerParams(
            dimension_semantics=("parallel", "parallel", "arbitrary"),
            vmem_limit_bytes=VMEM_LIMIT),
        name="attn",
    )(jnp.asarray(qi_tab), jnp.asarray(ki_tab), q, k, v)


def _conv_kernel(y_ref, halo_ref, w_ref, b_ref, lng_ref, lnb_ref, g_out_ref, o_ref, win_sc, *, ts, rc):
    i = pl.program_id(1)
    halo = halo_ref[...]
    win_sc[0:CONV_HALO, :] = jnp.where(i == 0, jnp.zeros_like(halo), halo)
    win_sc[CONV_HALO:CONV_HALO + ts, :] = y_ref[...]
    first = CONV_HALO - (CONV_K_ - 1)
    bias = b_ref[...]
    for r0 in range(0, ts, rc):
        acc = jnp.broadcast_to(bias, (rc, CONV_CH_))
        for kk in range(CONV_K_):
            acc = acc + w_ref[kk:kk + 1, :] * win_sc[r0 + first + kk:r0 + first + kk + rc, :]
        mu = jnp.mean(acc, axis=-1, keepdims=True)
        cen = acc - mu
        var = jnp.mean(cen * cen, axis=-1, keepdims=True)
        yn = cen * lax.rsqrt(var + EPS_) * lng_ref[...] + lnb_ref[...]
        act = yn * jax.nn.sigmoid(yn)
        o_ref[r0:r0 + rc, :] = _rms(act, g_out_ref[...]).astype(BF16)


def _conv_branch(glu, conv_w, conv_b, ln_g, ln_b, g_out, ts):
    b, s, _ = glu.shape
    rc = 32
    full = lambda shape: pl.BlockSpec(shape, lambda b_, i: (0,) * len(shape))
    per_halo = ts // CONV_HALO
    return pl.pallas_call(
        functools.partial(_conv_kernel, ts=ts, rc=rc),
        grid=(b, s // ts),
        in_specs=[
            pl.BlockSpec((None, ts, CONV_CH_), lambda b_, i: (b_, i, 0)),
            pl.BlockSpec((None, CONV_HALO, CONV_CH_),
                         lambda b_, i: (b_, jnp.maximum(i * per_halo - 1, 0), 0)),
            full((CONV_K_, CONV_CH_)), full((1, CONV_CH_)), full((1, CONV_CH_)),
            full((1, CONV_CH_)), full((1, CONV_CH_)),
        ],
        out_specs=pl.BlockSpec((None, ts, CONV_CH_), lambda b_, i: (b_, i, 0)),
        out_shape=jax.ShapeDtypeStruct((b, s, CONV_CH_), BF16),
        scratch_shapes=[pltpu.VMEM((CONV_HALO + ts, CONV_CH_), F32)],
        compiler_params=pltpu.CompilerParams(
            dimension_semantics=("parallel", "parallel"), vmem_limit_bytes=VMEM_LIMIT),
        name="conv",
    )(glu, glu, conv_w, conv_b, ln_g, ln_b, g_out)


def _outproj_kernel(x_ref, att_ref, cnv_ref, g_att_ref, w_att_ref, w_cnv_ref, g_ffn_ref,
                    h_ref, hn_ref):
    att_n = _rms(att_ref[...], g_att_ref[...]).astype(BF16)
    h = x_ref[...] + _dot(att_n, w_att_ref[...]) + _dot(cnv_ref[...], w_cnv_ref[...])
    h_ref[...] = h
    hn_ref[...] = _rms(h, g_ffn_ref[...]).astype(BF16)


def _outproj(x2, att, cnv, g_att, w_att, w_cnv, g_ffn, tm):
    t = x2.shape[0]
    full = lambda shape: pl.BlockSpec(shape, lambda i: (0,) * len(shape))
    return pl.pallas_call(
        _outproj_kernel,
        grid=(t // tm,),
        in_specs=[
            pl.BlockSpec((tm, D_MODEL_), lambda i: (i, 0)),
            pl.BlockSpec((tm, N_HEADS * V_HEAD_), lambda i: (i, 0)),
            pl.BlockSpec((tm, CONV_CH_), lambda i: (i, 0)),
            full((1, N_HEADS * V_HEAD_)), full(w_att.shape), full(w_cnv.shape), full((1, D_MODEL_)),
        ],
        out_specs=[
            pl.BlockSpec((tm, D_MODEL_), lambda i: (i, 0)),
            pl.BlockSpec((tm, D_MODEL_), lambda i: (i, 0)),
        ],
        out_shape=[
            jax.ShapeDtypeStruct((t, D_MODEL_), F32),
            jax.ShapeDtypeStruct((t, D_MODEL_), BF16),
        ],
        compiler_params=pltpu.CompilerParams(
            dimension_semantics=("parallel",), vmem_limit_bytes=VMEM_LIMIT),
        name="outproj",
    )(x2, att, cnv, g_att, w_att, w_cnv, g_ffn)


N_TOP = PEER_TOPK_ + 1
TOP_ROWS = 24
ROWQ = 16
PAIR = 2


def _sort_network(n):
    def merge(lo, hi, r):
        step = 2 * r
        if step < hi - lo:
            yield from merge(lo, hi, step)
            yield from merge(lo + r, hi, step)
            yield from ((i, i + r) for i in range(lo + r, hi - r, step))
        else:
            yield (lo, lo + r)

    def sort(lo, hi):
        if hi - lo >= 1:
            mid = lo + (hi - lo) // 2
            yield from sort(lo, mid)
            yield from sort(mid + 1, hi)
            yield from merge(lo, hi, 1)

    return list(sort(0, n - 1))


SORT16 = _sort_network(16)


def _extract_top(vals, n):
    tiles = [vals[8 * i:8 * (i + 1), :] for i in range(16)]
    for i, j in SORT16:
        tiles[i], tiles[j] = jnp.maximum(tiles[i], tiles[j]), jnp.minimum(tiles[i], tiles[j])
    out = []
    for k in range(n):
        m = jnp.max(tiles[0], axis=0, keepdims=True)
        out.append(m)
        hit = tiles[0] == m
        below = tiles[1:] + [jnp.full_like(m, -jnp.inf)]
        keep = min(n - k - 1, len(tiles))
        tiles = [jnp.where(hit, below[i], tiles[i]) for i in range(keep)]
    return out


def _peer_kernel(h_ref, xn_ref, wqT_ref, keys_ref, u_ref, vT_ref, o_ref,
                 s_sc, vtop_sc, e2_sc, thr_sc, a1_sc, at0_sc, at1_sc, c0_sc, c1_sc, outT_sc,
                 *, tb, ec):
    j = pl.program_id(1)
    ncol = tb // LANES
    n_i1 = ec // PEER_NKEYS_
    n_chunk = PEER_NKEYS_ // n_i1

    @pl.when(j == 0)
    def _scores_and_thresholds():
        qT = lax.dot_general(wqT_ref[...], xn_ref[...], NT_DIMS,
                             preferred_element_type=F32).astype(BF16)
        for hc in range(2 * PEER_HEADS_):
            sT = _dot(keys_ref[hc], qT[hc * PEER_DK_HALF_:(hc + 1) * PEER_DK_HALF_, :])
            for c in range(ncol):
                s_sc[hc, c] = sT[:, c * LANES:(c + 1) * LANES]
        vtop_sc[...] = jnp.full(vtop_sc.shape, -jnp.inf, F32)

        def thr_body(idx, carry):
            h = idx // ncol
            col = idx % ncol
            for c in range(2):
                for r, m in enumerate(_extract_top(s_sc[2 * h + c, col], N_TOP)):
                    vtop_sc[2 * h + c, col, r:r + 1, :] = m
            v1 = vtop_sc[2 * h, col]
            v2 = vtop_sc[2 * h + 1, col]
            blocks = [v1[0:1] + v2]
            blocks += [v1[a:a + 1] + v2[0:8] for a in range(1, 8)]
            blocks += [v1[8:TOP_ROWS] + v2[0:1]]
            blocks += [jnp.full((PEER_NKEYS_ - 4 * TOP_ROWS, LANES), -jnp.inf, F32)]
            cs = _extract_top(jnp.concatenate(blocks, axis=0), N_TOP)
            tau = 0.5 * (cs[PEER_TOPK_ - 1] + cs[PEER_TOPK_])
            z = jnp.exp(cs[0] - cs[0])
            for r in range(1, PEER_TOPK_):
                z = z + jnp.exp(cs[r] - cs[0])
            s1 = s_sc[2 * h, col]
            s2 = s_sc[2 * h + 1, col]
            e2_sc[h, col] = jnp.exp(s2 - v2[0:1])
            thr_sc[:, col, h] = (tau - s1).reshape(n_chunk, n_i1, LANES)
            a1_sc[:, col, h] = (jnp.exp(s1 - v1[0:1]) / z).reshape(n_chunk, n_i1, LANES)
            return carry

        lax.fori_loop(0, PEER_HEADS_ * ncol, thr_body, 0)
        outT_sc[...] = jnp.zeros(outT_sc.shape, F32)

    def zero_after(x):
        bits = lax.bitcast_convert_type(x, jnp.uint32)
        return lax.bitcast_convert_type((bits >> 16) >> 16, F32)

    def gate_column(at_src, c_dst, chunk, col, dep):
        for r0 in range(0, PEER_NKEYS_, ROWQ):
            zero = zero_after(dep)
            acc = [zero for _ in range(n_i1)]
            for h in range(PEER_HEADS_):
                s2 = s_sc[2 * h + 1, col, r0:r0 + ROWQ, :]
                e2 = e2_sc[h, col, r0:r0 + ROWQ, :]
                for i1l in range(n_i1):
                    thr = thr_sc[chunk, col, h, i1l:i1l + 1, :]
                    a1 = a1_sc[chunk, col, h, i1l:i1l + 1, :]
                    acc[i1l] = acc[i1l] + jnp.where(s2 >= thr, e2, 0.0) * a1
            for i1l in range(n_i1):
                rows = slice(i1l * PEER_NKEYS_ + r0, i1l * PEER_NKEYS_ + r0 + ROWQ)
                a = at_src[col, rows, :]
                gelu = 0.5 * a * (1.0 + lax.erf(a * np.float32(math.sqrt(0.5))))
                dep = gelu * acc[i1l]
                c_dst[rows, col * LANES:(col + 1) * LANES] = dep.astype(BF16)
        return dep

    hw = ec // 2
    dep = xn_ref[0:ROWQ, 0:LANES].astype(F32)
    for k, (at_sc, c_sc) in enumerate(((at0_sc, c0_sc), (at1_sc, c1_sc))):
        u_k = pltpu.bitcast(u_ref[k * hw:(k + 1) * hw, :], BF16)
        vt_k = pltpu.bitcast(vT_ref[:, k * ec:(k + 1) * ec], BF16)
        for n0 in range(0, tb, 2 * LANES):
            at = lax.dot_general(u_k, xn_ref[n0:n0 + 2 * LANES, :], NT_DIMS,
                                 preferred_element_type=F32)
            at_sc[n0 // LANES] = at[:, 0:LANES]
            at_sc[n0 // LANES + 1] = at[:, LANES:2 * LANES]
        for n0 in range(0, tb, 2 * LANES):
            for col in (n0 // LANES, n0 // LANES + 1):
                dep = gate_column(at_sc, c_sc, PAIR * j + k, col, dep)
            outT_sc[:, n0:n0 + 2 * LANES] += _dot(vt_k, c_sc[:, n0:n0 + 2 * LANES])

    @pl.when(j == pl.num_programs(1) - 1)
    def _fin():
        o_ref[...] = h_ref[...] + outT_sc[...].T


def _peer(h1, hn, wqT, keys, u_tab, vT_tab, tb, ec):
    t = h1.shape[0]
    n_exp = 2 * u_tab.shape[0]
    ncol = tb // LANES
    n_i1 = ec // PEER_NKEYS_
    n_pairs = n_exp // (PAIR * ec)
    return pl.pallas_call(
        functools.partial(_peer_kernel, tb=tb, ec=ec),
        grid=(t // tb, n_pairs),
        in_specs=[
            pl.BlockSpec((tb, D_MODEL_), lambda i, j: (i, 0)),
            pl.BlockSpec((tb, D_MODEL_), lambda i, j: (i, 0)),
            pl.BlockSpec(wqT.shape, lambda i, j: (0, 0)),
            pl.BlockSpec(keys.shape, lambda i, j: (0, 0, 0)),
            pl.BlockSpec((PAIR * ec // 2, D_MODEL_), lambda i, j: (j, 0)),
            pl.BlockSpec((D_MODEL_ // 2, PAIR * ec), lambda i, j: (0, j)),
        ],
        out_specs=pl.BlockSpec((tb, D_MODEL_), lambda i, j: (i, 0)),
        out_shape=jax.ShapeDtypeStruct((t, D_MODEL_), F32),
        scratch_shapes=[
            pltpu.VMEM((2 * PEER_HEADS_, ncol, PEER_NKEYS_, LANES), F32),
            pltpu.VMEM((2 * PEER_HEADS_, ncol, TOP_ROWS, LANES), F32),
            pltpu.VMEM((PEER_HEADS_, ncol, PEER_NKEYS_, LANES), F32),
            pltpu.VMEM((n_exp // ec, ncol, PEER_HEADS_, n_i1, LANES), F32),
            pltpu.VMEM((n_exp // ec, ncol, PEER_HEADS_, n_i1, LANES), F32),
            pltpu.VMEM((ncol, ec, LANES), F32),
            pltpu.VMEM((ncol, ec, LANES), F32),
            pltpu.VMEM((ec, tb), BF16),
            pltpu.VMEM((ec, tb), BF16),
            pltpu.VMEM((D_MODEL_, tb), F32),
        ],
        compiler_params=pltpu.CompilerParams(
            dimension_semantics=("parallel", "arbitrary"), vmem_limit_bytes=VMEM_LIMIT),
        name="peer",
    )(h1, hn, wqT, keys, u_tab, vT_tab)


def _ple_kernel(h_ref, p_ref, g_pl_ref, wg_ref, bg_ref, wp_ref, g_fin_ref, o_ref, *, last_layer):
    h = h_ref[...]
    gate = jax.nn.sigmoid(_dot(_rms(h, g_pl_ref[...]).astype(BF16), wg_ref[...]) + bg_ref[...])
    h = h + gate * _dot(p_ref[...].astype(BF16), wp_ref[...])
    o_ref[...] = _rms(h, g_fin_ref[...]) if last_layer else h


def _ple(h2, p2, g_pl, wg, bg, wp, g_fin, last_layer, tm):
    t = h2.shape[0]
    full = lambda shape: pl.BlockSpec(shape, lambda i: (0,) * len(shape))
    return pl.pallas_call(
        functools.partial(_ple_kernel, last_layer=last_layer),
        grid=(t // tm,),
        in_specs=[
            pl.BlockSpec((tm, D_MODEL_), lambda i: (i, 0)),
            pl.BlockSpec((tm, PLE_DIM_), lambda i: (i, 0)),
            full((1, D_MODEL_)), full(wg.shape), full((1, D_MODEL_)), full(wp.shape), full((1, D_MODEL_)),
        ],
        out_specs=pl.BlockSpec((tm, D_MODEL_), lambda i: (i, 0)),
        out_shape=jax.ShapeDtypeStruct((t, D_MODEL_), F32),
        compiler_params=pltpu.CompilerParams(
            dimension_semantics=("parallel",), vmem_limit_bytes=VMEM_LIMIT),
        name="ple",
    )(h2, p2, g_pl, wg, bg, wp, g_fin)


def _head_perm():
    pm = np.zeros((N_HEADS * (QK_NOPE_ + QK_ROPE_), N_HEADS * HEAD_PAD), np.float32)
    for h in range(N_HEADS):
        for d in range(QK_NOPE_):
            pm[h * QK_NOPE_ + d, h * HEAD_PAD + d] = 1.0
        for d in range(ROPE_HALF):
            pm[N_HEADS * QK_NOPE_ + h * ROPE_HALF + d, h * HEAD_PAD + QK_NOPE_ + d] = 1.0
            pm[N_HEADS * (QK_NOPE_ + ROPE_HALF) + h * ROPE_HALF + d,
               h * HEAD_PAD + QK_NOPE_ + ROPE_HALF + d] = 1.0
    return jnp.asarray(pm, BF16)


def _prep_w_in(w_in):
    q_lat = w_in[:, 0:Q_RANK_]
    kv_lat = w_in[:, Q_RANK_:Q_RANK_ + KV_RANK_]
    o = Q_RANK_ + KV_RANK_
    k1 = jnp.tile(w_in[:, o:o + ROPE_HALF], (1, N_HEADS))
    k2 = jnp.tile(w_in[:, o + ROPE_HALF:o + QK_ROPE_], (1, N_HEADS))
    conv = w_in[:, o + QK_ROPE_:]
    return jnp.concatenate([q_lat, kv_lat, k1, k2, conv], axis=1).astype(BF16)


def _prep_w_uq(w_uq):
    w = w_uq.reshape(Q_RANK_, N_HEADS, QK_NOPE_ + QK_ROPE_)
    nope = w[:, :, :QK_NOPE_].reshape(Q_RANK_, -1)
    r1 = w[:, :, QK_NOPE_:QK_NOPE_ + ROPE_HALF].reshape(Q_RANK_, -1)
    r2 = w[:, :, QK_NOPE_ + ROPE_HALF:].reshape(Q_RANK_, -1)
    return jnp.concatenate([nope, r1, r2], axis=1).astype(BF16)


def _prep_w_ukv(w_ukv):
    w = w_ukv.reshape(KV_RANK_, N_HEADS, QK_NOPE_ + V_HEAD_)
    k_nope = w[:, :, :QK_NOPE_].reshape(KV_RANK_, -1)
    v = w[:, :, QK_NOPE_:].reshape(KV_RANK_, -1)
    return jnp.concatenate([k_nope, v], axis=1).astype(BF16)


def _pack_row_pairs(x):
    r, c = x.shape
    return lax.bitcast_convert_type(x.reshape(r // 2, 2, c).transpose(0, 2, 1), jnp.uint32)


def _tile(n, want):
    return want if n % want == 0 else n


def kernel(x, p, positions, attn_norm, w_in, q_norm, w_uq, kv_norm, w_ukv, conv_w, conv_b, conv_ln_g, conv_ln_b, attn_out_norm, conv_out_norm, w_out, ffn_norm, peer_wq, peer_keys, peer_u, peer_v, pl_norm, pl_gate_w, pl_gate_b, pl_proj, final_norm):
    b, s, d = x.shape
    t = b * s
    depth = w_in.shape[0]
    tm = _tile(t, 512)
    tseq = _tile(s, 512)
    row = lambda a: a.reshape(1, -1)

    h = x.reshape(t, d)
    posf = positions.astype(F32).reshape(t, 1)
    half_freqs = ROPE_THETA_ ** (-jnp.arange(ROPE_HALF, dtype=F32) / ROPE_HALF)
    freq = jnp.tile(half_freqs, N_HEADS).reshape(1, LANES)
    perm = _head_perm()

    for i in range(depth):
        q, k, v, glu = _inproj(h, posf, freq, row(attn_norm[i]), _prep_w_in(w_in[i]), row(q_norm[i]),
                               _prep_w_uq(w_uq[i]), row(kv_norm[i]), _prep_w_ukv(w_ukv[i]), perm, tm)
        att = _attention(q.reshape(b, s, -1), k.reshape(b, s, -1), v.reshape(b, s, -1), tseq)
        cnv = _conv_branch(glu.reshape(b, s, -1), conv_w[i], row(conv_b[i]), row(conv_ln_g[i]),
                           row(conv_ln_b[i]), row(conv_out_norm[i]), tseq)
        w_o = w_out[i].astype(BF16)
        h1, hn = _outproj(h, att.reshape(t, -1), cnv.reshape(t, -1), row(attn_out_norm[i]),
                          w_o[:N_HEADS * V_HEAD_], w_o[N_HEADS * V_HEAD_:], row(ffn_norm[i]), tm)
        keys = peer_keys[i].reshape(2 * PEER_HEADS_, PEER_NKEYS_, PEER_DK_HALF_).astype(BF16)
        h2 = _peer(h1, hn, peer_wq[i].T.astype(BF16), keys, _pack_row_pairs(peer_u[i].astype(BF16)),
                   _pack_row_pairs(peer_v[i].T.astype(BF16)), tm, 1024)
        h = _ple(h2, p[i].reshape(t, -1), row(pl_norm[i]), pl_gate_w[i].astype(BF16),
                 row(pl_gate_b[i]), pl_proj[i].astype(BF16), row(final_norm), i == depth - 1, tm)
    return h.reshape(b, s, d)
```

```python
import functools
import math

import jax
import jax.numpy as jnp
import numpy as np
from jax import lax
from jax.experimental import pallas as pl
from jax.experimental.pallas import tpu as pltpu

F32 = jnp.float32
BF16 = jnp.bfloat16

D_MODEL_ = 1024
N_HEADS = 8
QK_NOPE_ = 64
QK_ROPE_ = 32
ROPE_HALF = QK_ROPE_ // 2
V_HEAD_ = 64
HEAD_PAD = 128
Q_RANK_ = 512
KV_RANK_ = 256
CONV_CH_ = 512
CONV_K_ = 31
CONV_HALO = 32
ROPE_THETA_ = 10000.0
PEER_HEADS_ = 8
PEER_NKEYS_ = 128
PEER_TOPK_ = 16
PEER_DK_HALF_ = 64
PLE_DIM_ = 256
EPS_ = 1e-6
NEG_BIG = -1e30
LANES = 128
SUBLANES = 8
VMEM_LIMIT = 56 * 1024 * 1024

NT_DIMS = (((1,), (1,)), ((), ()))
Q_PRESCALE = (QK_NOPE_ + QK_ROPE_) ** -0.5 * math.log2(math.e)


def _rms(x, g):
    return x * lax.rsqrt(jnp.mean(x * x, axis=-1, keepdims=True) + EPS_) * g


def _dot(a, b):
    return jnp.dot(a, b, preferred_element_type=F32)


def _inproj_kernel(x_ref, pos_ref, freq_ref, g_attn_ref, w_in_ref, g_q_ref, w_uq_ref,
                   g_kv_ref, w_ukv_ref, perm_ref, q_ref, k_ref, v_ref, glu_ref):
    x = x_ref[...]
    xn = _rms(x, g_attn_ref[...]).astype(BF16)
    z = _dot(xn, w_in_ref[...])
    q_lat = z[:, 0:512]
    kv_lat = z[:, 512:768]
    k1 = z[:, 768:896]
    k2 = z[:, 896:1024]
    glu_ref[...] = z[:, 1024:1536] * jax.nn.sigmoid(z[:, 1536:2048])

    q = _dot(_rms(q_lat, g_q_ref[...]).astype(BF16), w_uq_ref[...])
    kv = _dot(_rms(kv_lat, g_kv_ref[...]).astype(BF16), w_ukv_ref[...])
    ang = pos_ref[...] * freq_ref[...]
    cos = jnp.cos(ang)
    sin = jnp.sin(ang)
    p1 = q[:, 512:640]
    p2 = q[:, 640:768]
    qcat = jnp.concatenate([q[:, 0:512], p1 * cos - p2 * sin, p2 * cos + p1 * sin], axis=-1)
    qcat = qcat * np.float32(Q_PRESCALE)
    kcat = jnp.concatenate([kv[:, 0:512], k1 * cos - k2 * sin, k2 * cos + k1 * sin], axis=-1)
    q_ref[...] = _dot(qcat.astype(BF16), perm_ref[...]).astype(BF16)
    k_ref[...] = _dot(kcat.astype(BF16), perm_ref[...]).astype(BF16)
    v_ref[...] = kv[:, 512:1024].astype(BF16)


def _inproj(x2, posf, freq, g_attn, w_in_p, g_q, w_uq_p, g_kv, w_ukv_p, perm, tm):
    t = x2.shape[0]
    full = lambda shape: pl.BlockSpec(shape, lambda i: (0,) * len(shape))
    return pl.pallas_call(
        _inproj_kernel,
        grid=(t // tm,),
        in_specs=[
            pl.BlockSpec((tm, D_MODEL_), lambda i: (i, 0)),
            pl.BlockSpec((tm, 1), lambda i: (i, 0)),
            full((1, LANES)), full((1, D_MODEL_)), full(w_in_p.shape), full((1, Q_RANK_)),
            full(w_uq_p.shape), full((1, KV_RANK_)), full(w_ukv_p.shape), full(perm.shape),
        ],
        out_specs=[
            pl.BlockSpec((tm, N_HEADS * HEAD_PAD), lambda i: (i, 0)),
            pl.BlockSpec((tm, N_HEADS * HEAD_PAD), lambda i: (i, 0)),
            pl.BlockSpec((tm, N_HEADS * V_HEAD_), lambda i: (i, 0)),
            pl.BlockSpec((tm, CONV_CH_), lambda i: (i, 0)),
        ],
        out_shape=[
            jax.ShapeDtypeStruct((t, N_HEADS * HEAD_PAD), BF16),
            jax.ShapeDtypeStruct((t, N_HEADS * HEAD_PAD), BF16),
            jax.ShapeDtypeStruct((t, N_HEADS * V_HEAD_), BF16),
            jax.ShapeDtypeStruct((t, CONV_CH_), F32),
        ],
        compiler_params=pltpu.CompilerParams(
            dimension_semantics=("parallel",), vmem_limit_bytes=VMEM_LIMIT),
        name="inproj",
    )(x2, posf, freq, g_attn, w_in_p, g_q, w_uq_p, g_kv, w_ukv_p, perm)


def _attn_kernel(qi_ref, ki_ref, q_ref, k_ref, v_ref, o_ref, m_sc, l_sc, acc_sc, *, tq):
    t = pl.program_id(2)
    qi = qi_ref[t]
    ki = ki_ref[t]

    @pl.when(ki == 0)
    def _init():
        m_sc[...] = jnp.full(m_sc.shape, NEG_BIG, F32)
        l_sc[...] = jnp.zeros(l_sc.shape, F32)
        acc_sc[...] = jnp.zeros(acc_sc.shape, F32)

    def step(diagonal):
        v = v_ref[...]
        for hh in range(2):
            q = q_ref[:, hh * HEAD_PAD:(hh + 1) * HEAD_PAD]
            k = k_ref[:, hh * HEAD_PAD:(hh + 1) * HEAD_PAD]
            s = lax.dot_general(q, k, NT_DIMS, preferred_element_type=F32)
            if diagonal:
                row = lax.broadcasted_iota(jnp.int32, (tq, tq), 0)
                col = lax.broadcasted_iota(jnp.int32, (tq, tq), 1)
                s = jnp.where(col <= row, s, NEG_BIG)
            cols = [s[:, c * LANES:(c + 1) * LANES] for c in range(tq // LANES)]
            smax = functools.reduce(jnp.maximum, cols)
            m_prev = m_sc[hh]
            m_new = jnp.maximum(m_prev, jnp.max(smax, axis=-1, keepdims=True))
            alpha = jnp.exp2(m_prev - m_new)
            p_cols = [jnp.exp2(c - m_new) for c in cols]
            l_sc[hh] = alpha * l_sc[hh] + functools.reduce(jnp.add, p_cols)
            p = jnp.concatenate(p_cols, axis=-1).astype(BF16)
            acc_sc[hh] = alpha * acc_sc[hh] + _dot(p, v)
            m_sc[hh] = m_new

    @pl.when(ki < qi)
    def _full_tile():
        step(False)

    @pl.when(ki == qi)
    def _diagonal_tile():
        step(True)
        o0 = acc_sc[0] / jnp.sum(l_sc[0], axis=-1, keepdims=True)
        o1 = acc_sc[1] / jnp.sum(l_sc[1], axis=-1, keepdims=True)
        lane = lax.broadcasted_iota(jnp.int32, o0.shape, 1)
        o_ref[...] = jnp.where(lane < V_HEAD_, o0, o1)


def _attention(q, k, v, tq):
    b, s, _ = q.shape
    nq = s // tq
    qi_tab = np.asarray([qi for qi in range(nq) for _ in range(qi + 1)], np.int32)
    ki_tab = np.asarray([ki for qi in range(nq) for ki in range(qi + 1)], np.int32)
    grid_spec = pltpu.PrefetchScalarGridSpec(
        num_scalar_prefetch=2,
        grid=(b, N_HEADS // 2, len(qi_tab)),
        in_specs=[
            pl.BlockSpec((None, tq, 2 * HEAD_PAD), lambda b_, h, t, qt, kt: (b_, qt[t], h)),
            pl.BlockSpec((None, tq, 2 * HEAD_PAD), lambda b_, h, t, qt, kt: (b_, kt[t], h)),
            pl.BlockSpec((None, tq, 2 * V_HEAD_), lambda b_, h, t, qt, kt: (b_, kt[t], h)),
        ],
        out_specs=pl.BlockSpec((None, tq, 2 * V_HEAD_), lambda b_, h, t, qt, kt: (b_, qt[t], h)),
        scratch_shapes=[
            pltpu.VMEM((2, tq, LANES), F32),
            pltpu.VMEM((2, tq, LANES), F32),
            pltpu.VMEM((2, tq, 2 * V_HEAD_), F32),
        ],
    )
    return pl.pallas_call(
        functools.partial(_attn_kernel, tq=tq),
        grid_spec=grid_spec,
        out_shape=jax.ShapeDtypeStruct((b, s, N_HEADS * V_HEAD_), F32),
        compiler_params=pltpu.CompilerParams(
            dimension_semantics=("parallel", "parallel", "arbitrary"),
            vmem_limit_bytes=VMEM_LIMIT),
        name="attn",
    )(jnp.asarray(qi_tab), jnp.asarray(ki_tab), q, k, v)


def _conv_kernel(y_ref, halo_ref, w_ref, b_ref, lng_ref, lnb_ref, g_out_ref, o_ref, win_sc, *, ts, rc):
    i = pl.program_id(1)
    halo = halo_ref[...]
    win_sc[0, 0:CONV_HALO, :] = jnp.where(i == 0, jnp.zeros_like(halo), halo)
    win_sc[0, CONV_HALO:CONV_HALO + ts, :] = y_ref[...]
    span = CONV_HALO + ts - SUBLANES
    for r in range(1, SUBLANES):
        win_sc[r, 0:span, :] = win_sc[0, r:r + span, :]
    first = CONV_HALO - (CONV_K_ - 1)
    bias = b_ref[...]
    for r0 in range(0, ts, rc):
        acc = jnp.broadcast_to(bias, (rc, CONV_CH_))
        for kk in range(CONV_K_):
            a8, r = divmod(first + kk, SUBLANES)
            start = r0 + a8 * SUBLANES
            acc = acc + w_ref[kk:kk + 1, :] * win_sc[r, start:start + rc, :]
        mu = jnp.mean(acc, axis=-1, keepdims=True)
        cen = acc - mu
        var = jnp.mean(cen * cen, axis=-1, keepdims=True)
        yn = cen * lax.rsqrt(var + EPS_) * lng_ref[...] + lnb_ref[...]
        act = yn * jax.nn.sigmoid(yn)
        o_ref[r0:r0 + rc, :] = _rms(act, g_out_ref[...]).astype(BF16)


def _conv_branch(glu, conv_w, conv_b, ln_g, ln_b, g_out, ts):
    b, s, _ = glu.shape
    rc = 32
    full = lambda shape: pl.BlockSpec(shape, lambda b_, i: (0,) * len(shape))
    per_halo = ts // CONV_HALO
    return pl.pallas_call(
        functools.partial(_conv_kernel, ts=ts, rc=rc),
        grid=(b, s // ts),
        in_specs=[
            pl.BlockSpec((None, ts, CONV_CH_), lambda b_, i: (b_, i, 0)),
            pl.BlockSpec((None, CONV_HALO, CONV_CH_),
                         lambda b_, i: (b_, jnp.maximum(i * per_halo - 1, 0), 0)),
            full((CONV_K_, CONV_CH_)), full((1, CONV_CH_)), full((1, CONV_CH_)),
            full((1, CONV_CH_)), full((1, CONV_CH_)),
        ],
        out_specs=pl.BlockSpec((None, ts, CONV_CH_), lambda b_, i: (b_, i, 0)),
        out_shape=jax.ShapeDtypeStruct((b, s, CONV_CH_), BF16),
        scratch_shapes=[pltpu.VMEM((SUBLANES, CONV_HALO + ts, CONV_CH_), F32)],
        compiler_params=pltpu.CompilerParams(
            dimension_semantics=("parallel", "parallel"), vmem_limit_bytes=VMEM_LIMIT),
        name="conv",
    )(glu, glu, conv_w, conv_b, ln_g, ln_b, g_out)


def _outproj_kernel(x_ref, att_ref, cnv_ref, g_att_ref, w_att_ref, w_cnv_ref, g_ffn_ref,
                    h_ref, hn_ref):
    att_n = _rms(att_ref[...], g_att_ref[...]).astype(BF16)
    h = x_ref[...] + _dot(att_n, w_att_ref[...]) + _dot(cnv_ref[...], w_cnv_ref[...])
    h_ref[...] = h
    hn_ref[...] = _rms(h, g_ffn_ref[...]).astype(BF16)


def _outproj(x2, att, cnv, g_att, w_att, w_cnv, g_ffn, tm):
    t = x2.shape[0]
    full = lambda shape: pl.BlockSpec(shape, lambda i: (0,) * len(shape))
    return pl.pallas_call(
        _outproj_kernel,
        grid=(t // tm,),
        in_specs=[
            pl.BlockSpec((tm, D_MODEL_), lambda i: (i, 0)),
            pl.BlockSpec((tm, N_HEADS * V_HEAD_), lambda i: (i, 0)),
            pl.BlockSpec((tm, CONV_CH_), lambda i: (i, 0)),
            full((1, N_HEADS * V_HEAD_)), full(w_att.shape), full(w_cnv.shape), full((1, D_MODEL_)),
        ],
        out_specs=[
            pl.BlockSpec((tm, D_MODEL_), lambda i: (i, 0)),
            pl.BlockSpec((tm, D_MODEL_), lambda i: (i, 0)),
        ],
        out_shape=[
            jax.ShapeDtypeStruct((t, D_MODEL_), F32),
            jax.ShapeDtypeStruct((t, D_MODEL_), BF16),
        ],
        compiler_params=pltpu.CompilerParams(
            dimension_semantics=("parallel",), vmem_limit_bytes=VMEM_LIMIT),
        name="outproj",
    )(x2, att, cnv, g_att, w_att, w_cnv, g_ffn)


N_TOP = PEER_TOPK_ + 1
TOP_ROWS = 24
ROWQ = 16
PAIR = 2


def _sort_network(n):
    def merge(lo, hi, r):
        step = 2 * r
        if step < hi - lo:
            yield from merge(lo, hi, step)
            yield from merge(lo + r, hi, step)
            yield from ((i, i + r) for i in range(lo + r, hi - r, step))
        else:
            yield (lo, lo + r)

    def sort(lo, hi):
        if hi - lo >= 1:
            mid = lo + (hi - lo) // 2
            yield from sort(lo, mid)
            yield from sort(mid + 1, hi)
            yield from merge(lo, hi, 1)

    return list(sort(0, n - 1))


SORT16 = _sort_network(16)


def _extract_top(vals, n):
    tiles = [vals[8 * i:8 * (i + 1), :] for i in range(16)]
    for i, j in SORT16:
        tiles[i], tiles[j] = jnp.maximum(tiles[i], tiles[j]), jnp.minimum(tiles[i], tiles[j])
    out = []
    for k in range(n):
        m = jnp.max(tiles[0], axis=0, keepdims=True)
        out.append(m)
        hit = tiles[0] == m
        below = tiles[1:] + [jnp.full_like(m, -jnp.inf)]
        keep = min(n - k - 1, len(tiles))
        tiles = [jnp.where(hit, below[i], tiles[i]) for i in range(keep)]
    return out


def _peer_kernel(h_ref, xn_ref, wqT_ref, keys_ref, u_ref, vT_ref, o_ref,
                 s_sc, vtop_sc, e2_sc, thr_sc, a1_sc, at0_sc, at1_sc, c0_sc, c1_sc, outT_sc,
                 *, tb, ec):
    j = pl.program_id(1)
    ncol = tb // LANES
    n_i1 = ec // PEER_NKEYS_
    n_chunk = PEER_NKEYS_ // n_i1

    @pl.when(j == 0)
    def _scores_and_thresholds():
        qT = lax.dot_general(wqT_ref[...], xn_ref[...], NT_DIMS,
                             preferred_element_type=F32).astype(BF16)
        for hc in range(2 * PEER_HEADS_):
            sT = _dot(keys_ref[hc], qT[hc * PEER_DK_HALF_:(hc + 1) * PEER_DK_HALF_, :])
            for c in range(ncol):
                s_sc[hc, c] = sT[:, c * LANES:(c + 1) * LANES]
        vtop_sc[...] = jnp.full(vtop_sc.shape, -jnp.inf, F32)

        def thr_body(idx, carry):
            h = idx // ncol
            col = idx % ncol
            for c in range(2):
                for r, m in enumerate(_extract_top(s_sc[2 * h + c, col], N_TOP)):
                    vtop_sc[2 * h + c, col, r:r + 1, :] = m
            v1 = vtop_sc[2 * h, col]
            v2 = vtop_sc[2 * h + 1, col]
            blocks = [v1[0:1] + v2]
            blocks += [v1[a:a + 1] + v2[0:8] for a in range(1, 8)]
            blocks += [v1[8:TOP_ROWS] + v2[0:1]]
            blocks += [jnp.full((PEER_NKEYS_ - 4 * TOP_ROWS, LANES), -jnp.inf, F32)]
            cs = _extract_top(jnp.concatenate(blocks, axis=0), N_TOP)
            tau = 0.5 * (cs[PEER_TOPK_ - 1] + cs[PEER_TOPK_])
            z = jnp.exp(cs[0] - cs[0])
            for r in range(1, PEER_TOPK_):
                z = z + jnp.exp(cs[r] - cs[0])
            s1 = s_sc[2 * h, col]
            s2 = s_sc[2 * h + 1, col]
            e2_sc[h, col] = jnp.exp(s2 - v2[0:1])
            thr_sc[:, col, h] = (tau - s1).reshape(n_chunk, n_i1, LANES)
            a1_sc[:, col, h] = (jnp.exp(s1 - v1[0:1]) / z).reshape(n_chunk, n_i1, LANES)
            return carry

        lax.fori_loop(0, PEER_HEADS_ * ncol, thr_body, 0)
        outT_sc[...] = jnp.zeros(outT_sc.shape, F32)

    def zero_after(x):
        bits = lax.bitcast_convert_type(x, jnp.uint32)
        return lax.bitcast_convert_type((bits >> 16) >> 16, F32)

    def gate_column(at_src, c_dst, chunk, col, dep):
        for r0 in range(0, PEER_NKEYS_, ROWQ):
            zero = zero_after(dep)
            acc = [zero for _ in range(n_i1)]
            for h in range(PEER_HEADS_):
                s2 = s_sc[2 * h + 1, col, r0:r0 + ROWQ, :]
                e2 = e2_sc[h, col, r0:r0 + ROWQ, :]
                for i1l in range(n_i1):
                    thr = thr_sc[chunk, col, h, i1l:i1l + 1, :]
                    a1 = a1_sc[chunk, col, h, i1l:i1l + 1, :]
                    acc[i1l] = acc[i1l] + jnp.where(s2 >= thr, e2, 0.0) * a1
            for i1l in range(n_i1):
                rows = slice(i1l * PEER_NKEYS_ + r0, i1l * PEER_NKEYS_ + r0 + ROWQ)
                a = at_src[col, rows, :]
                gelu = 0.5 * a * (1.0 + lax.erf(a * np.float32(math.sqrt(0.5))))
                dep = gelu * acc[i1l]
                c_dst[rows, col * LANES:(col + 1) * LANES] = dep.astype(BF16)
        return dep

    hw = ec // 2
    dep = xn_ref[0:ROWQ, 0:LANES].astype(F32)
    for k, (at_sc, c_sc) in enumerate(((at0_sc, c0_sc), (at1_sc, c1_sc))):
        u_k = pltpu.bitcast(u_ref[k * hw:(k + 1) * hw, :], BF16)
        vt_k = pltpu.bitcast(vT_ref[:, k * ec:(k + 1) * ec], BF16)
        for n0 in range(0, tb, 2 * LANES):
            at = lax.dot_general(u_k, xn_ref[n0:n0 + 2 * LANES, :], NT_DIMS,
                                 preferred_element_type=F32)
            at_sc[n0 // LANES] = at[:, 0:LANES]
            at_sc[n0 // LANES + 1] = at[:, LANES:2 * LANES]
        for n0 in range(0, tb, 2 * LANES):
            for col in (n0 // LANES, n0 // LANES + 1):
                dep = gate_column(at_sc, c_sc, PAIR * j + k, col, dep)
            outT_sc[:, n0:n0 + 2 * LANES] += _dot(vt_k, c_sc[:, n0:n0 + 2 * LANES])

    @pl.when(j == pl.num_programs(1) - 1)
    def _fin():
        o_ref[...] = h_ref[...] + outT_sc[...].T


def _peer(h1, hn, wqT, keys, u_tab, vT_tab, tb, ec):
    t = h1.shape[0]
    n_exp = 2 * u_tab.shape[0]
    ncol = tb // LANES
    n_i1 = ec // PEER_NKEYS_
    n_pairs = n_exp // (PAIR * ec)
    return pl.pallas_call(
        functools.partial(_peer_kernel, tb=tb, ec=ec),
        grid=(t // tb, n_pairs),
        in_specs=[
            pl.BlockSpec((tb, D_MODEL_), lambda i, j: (i, 0)),
            pl.BlockSpec((tb, D_MODEL_), lambda i, j: (i, 0)),
            pl.BlockSpec(wqT.shape, lambda i, j: (0, 0)),
            pl.BlockSpec(keys.shape, lambda i, j: (0, 0, 0)),
            pl.BlockSpec((PAIR * ec // 2, D_MODEL_), lambda i, j: (j, 0)),
            pl.BlockSpec((D_MODEL_ // 2, PAIR * ec), lambda i, j: (0, j)),
        ],
        out_specs=pl.BlockSpec((tb, D_MODEL_), lambda i, j: (i, 0)),
        out_shape=jax.ShapeDtypeStruct((t, D_MODEL_), F32),
        scratch_shapes=[
            pltpu.VMEM((2 * PEER_HEADS_, ncol, PEER_NKEYS_, LANES), F32),
            pltpu.VMEM((2 * PEER_HEADS_, ncol, TOP_ROWS, LANES), F32),
            pltpu.VMEM((PEER_HEADS_, ncol, PEER_NKEYS_, LANES), F32),
            pltpu.VMEM((n_exp // ec, ncol, PEER_HEADS_, n_i1, LANES), F32),
            pltpu.VMEM((n_exp // ec, ncol, PEER_HEADS_, n_i1, LANES), F32),
            pltpu.VMEM((ncol, ec, LANES), F32),
            pltpu.VMEM((ncol, ec, LANES), F32),
            pltpu.VMEM((ec, tb), BF16),
            pltpu.VMEM((ec, tb), BF16),
            pltpu.VMEM((D_MODEL_, tb), F32),
        ],
        compiler_params=pltpu.CompilerParams(
            dimension_semantics=("parallel", "arbitrary"), vmem_limit_bytes=VMEM_LIMIT),
        name="peer",
    )(h1, hn, wqT, keys, u_tab, vT_tab)


def _ple_kernel(h_ref, p_ref, g_pl_ref, wg_ref, bg_ref, wp_ref, g_fin_ref, o_ref, *, last_layer):
    h = h_ref[...]
    gate = jax.nn.sigmoid(_dot(_rms(h, g_pl_ref[...]).astype(BF16), wg_ref[...]) + bg_ref[...])
    h = h + gate * _dot(p_ref[...].astype(BF16), wp_ref[...])
    o_ref[...] = _rms(h, g_fin_ref[...]) if last_layer else h


def _ple(h2, p2, g_pl, wg, bg, wp, g_fin, last_layer, tm):
    t = h2.shape[0]
    full = lambda shape: pl.BlockSpec(shape, lambda i: (0,) * len(shape))
    return pl.pallas_call(
        functools.partial(_ple_kernel, last_layer=last_layer),
        grid=(t // tm,),
        in_specs=[
            pl.BlockSpec((tm, D_MODEL_), lambda i: (i, 0)),
            pl.BlockSpec((tm, PLE_DIM_), lambda i: (i, 0)),
            full((1, D_MODEL_)), full(wg.shape), full((1, D_MODEL_)), full(wp.shape), full((1, D_MODEL_)),
        ],
        out_specs=pl.BlockSpec((tm, D_MODEL_), lambda i: (i, 0)),
        out_shape=jax.ShapeDtypeStruct((t, D_MODEL_), F32),
        compiler_params=pltpu.CompilerParams(
            dimension_semantics=("parallel",), vmem_limit_bytes=VMEM_LIMIT),
        name="ple",
    )(h2, p2, g_pl, wg, bg, wp, g_fin)


def _head_perm():
    pm = np.zeros((N_HEADS * (QK_NOPE_ + QK_ROPE_), N_HEADS * HEAD_PAD), np.float32)
    for h in range(N_HEADS):
        for d in range(QK_NOPE_):
            pm[h * QK_NOPE_ + d, h * HEAD_PAD + d] = 1.0
        for d in range(ROPE_HALF):
            pm[N_HEADS * QK_NOPE_ + h * ROPE_HALF + d, h * HEAD_PAD + QK_NOPE_ + d] = 1.0
            pm[N_HEADS * (QK_NOPE_ + ROPE_HALF) + h * ROPE_HALF + d,
               h * HEAD_PAD + QK_NOPE_ + ROPE_HALF + d] = 1.0
    return jnp.asarray(pm, BF16)


def _prep_w_in(w_in):
    q_lat = w_in[:, 0:Q_RANK_]
    kv_lat = w_in[:, Q_RANK_:Q_RANK_ + KV_RANK_]
    o = Q_RANK_ + KV_RANK_
    k1 = jnp.tile(w_in[:, o:o + ROPE_HALF], (1, N_HEADS))
    k2 = jnp.tile(w_in[:, o + ROPE_HALF:o + QK_ROPE_], (1, N_HEADS))
    conv = w_in[:, o + QK_ROPE_:]
    return jnp.concatenate([q_lat, kv_lat, k1, k2, conv], axis=1).astype(BF16)


def _prep_w_uq(w_uq):
    w = w_uq.reshape(Q_RANK_, N_HEADS, QK_NOPE_ + QK_ROPE_)
    nope = w[:, :, :QK_NOPE_].reshape(Q_RANK_, -1)
    r1 = w[:, :, QK_NOPE_:QK_NOPE_ + ROPE_HALF].reshape(Q_RANK_, -1)
    r2 = w[:, :, QK_NOPE_ + ROPE_HALF:].reshape(Q_RANK_, -1)
    return jnp.concatenate([nope, r1, r2], axis=1).astype(BF16)


def _prep_w_ukv(w_ukv):
    w = w_ukv.reshape(KV_RANK_, N_HEADS, QK_NOPE_ + V_HEAD_)
    k_nope = w[:, :, :QK_NOPE_].reshape(KV_RANK_, -1)
    v = w[:, :, QK_NOPE_:].reshape(KV_RANK_, -1)
    return jnp.concatenate([k_nope, v], axis=1).astype(BF16)


def _pack_u_kernel(x_ref, o_ref):
    o_ref[...] = pltpu.bitcast(x_ref[...].astype(BF16), jnp.uint32)


def _pack_vt_kernel(x_ref, o_ref):
    o_ref[...] = pltpu.bitcast(x_ref[...].T.astype(BF16), jnp.uint32)


def _pack_tables(u_tab, v_tab, rows):
    n, d = u_tab.shape
    params = pltpu.CompilerParams(dimension_semantics=("parallel",), vmem_limit_bytes=VMEM_LIMIT)
    u_pk = pl.pallas_call(
        _pack_u_kernel, grid=(n // rows,),
        in_specs=[pl.BlockSpec((rows, d), lambda i: (i, 0))],
        out_specs=pl.BlockSpec((rows // 2, d), lambda i: (i, 0)),
        out_shape=jax.ShapeDtypeStruct((n // 2, d), jnp.uint32),
        compiler_params=params, name="pack_u")(u_tab)
    vt_pk = pl.pallas_call(
        _pack_vt_kernel, grid=(n // rows,),
        in_specs=[pl.BlockSpec((rows, d), lambda i: (i, 0))],
        out_specs=pl.BlockSpec((d // 2, rows), lambda i: (0, i)),
        out_shape=jax.ShapeDtypeStruct((d // 2, n), jnp.uint32),
        compiler_params=params, name="pack_vt")(v_tab)
    return u_pk, vt_pk


def _tile(n, want):
    return want if n % want == 0 else n


def kernel(x, p, positions, attn_norm, w_in, q_norm, w_uq, kv_norm, w_ukv, conv_w, conv_b, conv_ln_g, conv_ln_b, attn_out_norm, conv_out_norm, w_out, ffn_norm, peer_wq, peer_keys, peer_u, peer_v, pl_norm, pl_gate_w, pl_gate_b, pl_proj, final_norm):
    b, s, d = x.shape
    t = b * s
    depth = w_in.shape[0]
    tm = _tile(t, 512)
    tseq = _tile(s, 512)
    row = lambda a: a.reshape(1, -1)

    h = x.reshape(t, d)
    posf = positions.astype(F32).reshape(t, 1)
    half_freqs = ROPE_THETA_ ** (-jnp.arange(ROPE_HALF, dtype=F32) / ROPE_HALF)
    freq = jnp.tile(half_freqs, N_HEADS).reshape(1, LANES)
    perm = _head_perm()

    for i in range(depth):
        q, k, v, glu = _inproj(h, posf, freq, row(attn_norm[i]), _prep_w_in(w_in[i]), row(q_norm[i]),
                               _prep_w_uq(w_uq[i]), row(kv_norm[i]), _prep_w_ukv(w_ukv[i]), perm, tm)
        att = _attention(q.reshape(b, s, -1), k.reshape(b, s, -1), v.reshape(b, s, -1), tseq)
        cnv = _conv_branch(glu.reshape(b, s, -1), conv_w[i], row(conv_b[i]), row(conv_ln_g[i]),
                           row(conv_ln_b[i]), row(conv_out_norm[i]), tseq)
        w_o = w_out[i].astype(BF16)
        h1, hn = _outproj(h, att.reshape(t, -1), cnv.reshape(t, -1), row(attn_out_norm[i]),
                          w_o[:N_HEADS * V_HEAD_], w_o[N_HEADS * V_HEAD_:], row(ffn_norm[i]), tm)
        keys = peer_keys[i].reshape(2 * PEER_HEADS_, PEER_NKEYS_, PEER_DK_HALF_).astype(BF16)
        u_pk, vt_pk = _pack_tables(peer_u[i], peer_v[i], 512)
        h2 = _peer(h1, hn, peer_wq[i].T.astype(BF16), keys, u_pk, vt_pk, tm, 1024)
        h = _ple(h2, p[i].reshape(t, -1), row(pl_norm[i]), pl_gate_w[i].astype(BF16),
                 row(pl_gate_b[i]), pl_proj[i].astype(BF16), row(final_norm), i == depth - 1, tm)
    return h.reshape(b, s, d)
```

```python
import functools
import math

import jax
import jax.numpy as jnp
import numpy as np
from jax import lax
from jax.experimental import pallas as pl
from jax.experimental.pallas import tpu as pltpu

F32 = jnp.float32
BF16 = jnp.bfloat16

D_MODEL_ = 1024
N_HEADS = 8
QK_NOPE_ = 64
QK_ROPE_ = 32
ROPE_HALF = QK_ROPE_ // 2
V_HEAD_ = 64
HEAD_PAD = 128
Q_RANK_ = 512
KV_RANK_ = 256
CONV_CH_ = 512
CONV_K_ = 31
CONV_HALO = 32
ROPE_THETA_ = 10000.0
PEER_HEADS_ = 8
PEER_NKEYS_ = 128
PEER_TOPK_ = 16
PEER_DK_HALF_ = 64
PLE_DIM_ = 256
EPS_ = 1e-6
NEG_BIG = -1e30
LANES = 128
SUBLANES = 8
VMEM_LIMIT = 56 * 1024 * 1024

NT_DIMS = (((1,), (1,)), ((), ()))
Q_PRESCALE = (QK_NOPE_ + QK_ROPE_) ** -0.5 * math.log2(math.e)


def _rms(x, g):
    return x * lax.rsqrt(jnp.mean(x * x, axis=-1, keepdims=True) + EPS_) * g


def _dot(a, b):
    return jnp.dot(a, b, preferred_element_type=F32)


def _inproj_kernel(x_ref, pos_ref, freq_ref, g_attn_ref, w_in_ref, g_q_ref, w_uq_ref,
                   g_kv_ref, w_ukv_ref, perm_ref, q_ref, k_ref, v_ref, glu_ref):
    x = x_ref[...]
    xn = _rms(x, g_attn_ref[...]).astype(BF16)
    z = _dot(xn, w_in_ref[...])
    q_lat = z[:, 0:512]
    kv_lat = z[:, 512:768]
    k1 = z[:, 768:896]
    k2 = z[:, 896:1024]
    glu_ref[...] = z[:, 1024:1536] * jax.nn.sigmoid(z[:, 1536:2048])

    q = _dot(_rms(q_lat, g_q_ref[...]).astype(BF16), w_uq_ref[...])
    kv = _dot(_rms(kv_lat, g_kv_ref[...]).astype(BF16), w_ukv_ref[...])
    ang = pos_ref[...] * freq_ref[...]
    cos = jnp.cos(ang)
    sin = jnp.sin(ang)
    p1 = q[:, 512:640]
    p2 = q[:, 640:768]
    qcat = jnp.concatenate([q[:, 0:512], p1 * cos - p2 * sin, p2 * cos + p1 * sin], axis=-1)
    qcat = qcat * np.float32(Q_PRESCALE)
    kcat = jnp.concatenate([kv[:, 0:512], k1 * cos - k2 * sin, k2 * cos + k1 * sin], axis=-1)
    q_ref[...] = _dot(qcat.astype(BF16), perm_ref[...]).astype(BF16)
    k_ref[...] = _dot(kcat.astype(BF16), perm_ref[...]).astype(BF16)
    v_ref[...] = kv[:, 512:1024].astype(BF16)


def _inproj(x2, posf, freq, g_attn, w_in_p, g_q, w_uq_p, g_kv, w_ukv_p, perm, tm):
    t = x2.shape[0]
    full = lambda shape: pl.BlockSpec(shape, lambda i: (0,) * len(shape))
    return pl.pallas_call(
        _inproj_kernel,
        grid=(t // tm,),
        in_specs=[
            pl.BlockSpec((tm, D_MODEL_), lambda i: (i, 0)),
            pl.BlockSpec((tm, 1), lambda i: (i, 0)),
            full((1, LANES)), full((1, D_MODEL_)), full(w_in_p.shape), full((1, Q_RANK_)),
            full(w_uq_p.shape), full((1, KV_RANK_)), full(w_ukv_p.shape), full(perm.shape),
        ],
        out_specs=[
            pl.BlockSpec((tm, N_HEADS * HEAD_PAD), lambda i: (i, 0)),
            pl.BlockSpec((tm, N_HEADS * HEAD_PAD), lambda i: (i, 0)),
            pl.BlockSpec((tm, N_HEADS * V_HEAD_), lambda i: (i, 0)),
            pl.BlockSpec((tm, CONV_CH_), lambda i: (i, 0)),
        ],
        out_shape=[
            jax.ShapeDtypeStruct((t, N_HEADS * HEAD_PAD), BF16),
            jax.ShapeDtypeStruct((t, N_HEADS * HEAD_PAD), BF16),
            jax.ShapeDtypeStruct((t, N_HEADS * V_HEAD_), BF16),
            jax.ShapeDtypeStruct((t, CONV_CH_), F32),
        ],
        compiler_params=pltpu.CompilerParams(
            dimension_semantics=("parallel",), vmem_limit_bytes=VMEM_LIMIT),
        name="inproj",
    )(x2, posf, freq, g_attn, w_in_p, g_q, w_uq_p, g_kv, w_ukv_p, perm)


def _attn_kernel(qi_ref, ki_ref, q_ref, k_ref, v_ref, o_ref, m_sc, l_sc, acc_sc, *, tq):
    t = pl.program_id(2)
    qi = qi_ref[t]
    ki = ki_ref[t]

    @pl.when(ki == 0)
    def _init():
        m_sc[...] = jnp.full(m_sc.shape, NEG_BIG, F32)
        l_sc[...] = jnp.zeros(l_sc.shape, F32)
        acc_sc[...] = jnp.zeros(acc_sc.shape, F32)

    def step(diagonal):
        v = v_ref[...]
        for hh in range(2):
            q = q_ref[:, hh * HEAD_PAD:(hh + 1) * HEAD_PAD]
            k = k_ref[:, hh * HEAD_PAD:(hh + 1) * HEAD_PAD]
            s = lax.dot_general(q, k, NT_DIMS, preferred_element_type=F32)
            if diagonal:
                row = lax.broadcasted_iota(jnp.int32, (tq, tq), 0)
                col = lax.broadcasted_iota(jnp.int32, (tq, tq), 1)
                s = jnp.where(col <= row, s, NEG_BIG)
            cols = [s[:, c * LANES:(c + 1) * LANES] for c in range(tq // LANES)]
            smax = functools.reduce(jnp.maximum, cols)
            m_prev = m_sc[hh]
            m_new = jnp.maximum(m_prev, jnp.max(smax, axis=-1, keepdims=True))
            alpha = jnp.exp2(m_prev - m_new)
            p_cols = [jnp.exp2(c - m_new) for c in cols]
            l_sc[hh] = alpha * l_sc[hh] + functools.reduce(jnp.add, p_cols)
            p = jnp.concatenate(p_cols, axis=-1).astype(BF16)
            acc_sc[hh] = alpha * acc_sc[hh] + _dot(p, v)
            m_sc[hh] = m_new

    @pl.when(ki < qi)
    def _full_tile():
        step(False)

    @pl.when(ki == qi)
    def _diagonal_tile():
        step(True)
        o0 = acc_sc[0] / jnp.sum(l_sc[0], axis=-1, keepdims=True)
        o1 = acc_sc[1] / jnp.sum(l_sc[1], axis=-1, keepdims=True)
        lane = lax.broadcasted_iota(jnp.int32, o0.shape, 1)
        o_ref[...] = jnp.where(lane < V_HEAD_, o0, o1)


def _attention(q, k, v, tq):
    b, s, _ = q.shape
    nq = s // tq
    qi_tab = np.asarray([qi for qi in range(nq) for _ in range(qi + 1)], np.int32)
    ki_tab = np.asarray([ki for qi in range(nq) for ki in range(qi + 1)], np.int32)
    grid_spec = pltpu.PrefetchScalarGridSpec(
        num_scalar_prefetch=2,
        grid=(b, N_HEADS // 2, len(qi_tab)),
        in_specs=[
            pl.BlockSpec((None, tq, 2 * HEAD_PAD), lambda b_, h, t, qt, kt: (b_, qt[t], h)),
            pl.BlockSpec((None, tq, 2 * HEAD_PAD), lambda b_, h, t, qt, kt: (b_, kt[t], h)),
            pl.BlockSpec((None, tq, 2 * V_HEAD_), lambda b_, h, t, qt, kt: (b_, kt[t], h)),
        ],
        out_specs=pl.BlockSpec((None, tq, 2 * V_HEAD_), lambda b_, h, t, qt, kt: (b_, qt[t], h)),
        scratch_shapes=[
            pltpu.VMEM((2, tq, LANES), F32),
            pltpu.VMEM((2, tq, LANES), F32),
            pltpu.VMEM((2, tq, 2 * V_HEAD_), F32),
        ],
    )
    return pl.pallas_call(
        functools.partial(_attn_kernel, tq=tq),
        grid_spec=grid_spec,
        out_shape=jax.ShapeDtypeStruct((b, s, N_HEADS * V_HEAD_), F32),
        compiler_params=pltpu.CompilerParams(
            dimension_semantics=("parallel", "parallel", "arbitrary"),
            vmem_limit_bytes=VMEM_LIMIT),
        name="attn",
    )(jnp.asarray(qi_tab), jnp.asarray(ki_tab), q, k, v)


def _conv_kernel(y_ref, halo_ref, w_ref, b_ref, lng_ref, lnb_ref, g_out_ref, o_ref, win_sc, *, ts, rc):
    i = pl.program_id(1)
    halo = halo_ref[...]
    win_sc[0, 0:CONV_HALO, :] = jnp.where(i == 0, jnp.zeros_like(halo), halo)
    win_sc[0, CONV_HALO:CONV_HALO + ts, :] = y_ref[...]
    span = CONV_HALO + ts - SUBLANES
    for r in range(1, SUBLANES):
        win_sc[r, 0:span, :] = win_sc[0, r:r + span, :]
    first = CONV_HALO - (CONV_K_ - 1)
    bias = b_ref[...]
    for r0 in range(0, ts, rc):
        acc = jnp.broadcast_to(bias, (rc, CONV_CH_))
        for kk in range(CONV_K_):
            a8, r = divmod(first + kk, SUBLANES)
            start = r0 + a8 * SUBLANES
            acc = acc + w_ref[kk:kk + 1, :] * win_sc[r, start:start + rc, :]
        mu = jnp.mean(acc, axis=-1, keepdims=True)
        cen = acc - mu
        var = jnp.mean(cen * cen, axis=-1, keepdims=True)
        yn = cen * lax.rsqrt(var + EPS_) * lng_ref[...] + lnb_ref[...]
        act = yn * jax.nn.sigmoid(yn)
        o_ref[r0:r0 + rc, :] = _rms(act, g_out_ref[...]).astype(BF16)


def _conv_branch(glu, conv_w, conv_b, ln_g, ln_b, g_out, ts):
    b, s, _ = glu.shape
    rc = 32
    full = lambda shape: pl.BlockSpec(shape, lambda b_, i: (0,) * len(shape))
    per_halo = ts // CONV_HALO
    return pl.pallas_call(
        functools.partial(_conv_kernel, ts=ts, rc=rc),
        grid=(b, s // ts),
        in_specs=[
            pl.BlockSpec((None, ts, CONV_CH_), lambda b_, i: (b_, i, 0)),
            pl.BlockSpec((None, CONV_HALO, CONV_CH_),
                         lambda b_, i: (b_, jnp.maximum(i * per_halo - 1, 0), 0)),
            full((CONV_K_, CONV_CH_)), full((1, CONV_CH_)), full((1, CONV_CH_)),
            full((1, CONV_CH_)), full((1, CONV_CH_)),
        ],
        out_specs=pl.BlockSpec((None, ts, CONV_CH_), lambda b_, i: (b_, i, 0)),
        out_shape=jax.ShapeDtypeStruct((b, s, CONV_CH_), BF16),
        scratch_shapes=[pltpu.VMEM((SUBLANES, CONV_HALO + ts, CONV_CH_), F32)],
        compiler_params=pltpu.CompilerParams(
            dimension_semantics=("parallel", "parallel"), vmem_limit_bytes=VMEM_LIMIT),
        name="conv",
    )(glu, glu, conv_w, conv_b, ln_g, ln_b, g_out)


def _outproj_kernel(x_ref, att_ref, cnv_ref, g_att_ref, w_att_ref, w_cnv_ref, g_ffn_ref,
                    h_ref, hn_ref):
    att_n = _rms(att_ref[...], g_att_ref[...]).astype(BF16)
    h = x_ref[...] + _dot(att_n, w_att_ref[...]) + _dot(cnv_ref[...], w_cnv_ref[...])
    h_ref[...] = h
    hn_ref[...] = _rms(h, g_ffn_ref[...]).astype(BF16)


def _outproj(x2, att, cnv, g_att, w_att, w_cnv, g_ffn, tm):
    t = x2.shape[0]
    full = lambda shape: pl.BlockSpec(shape, lambda i: (0,) * len(shape))
    return pl.pallas_call(
        _outproj_kernel,
        grid=(t // tm,),
        in_specs=[
            pl.BlockSpec((tm, D_MODEL_), lambda i: (i, 0)),
            pl.BlockSpec((tm, N_HEADS * V_HEAD_), lambda i: (i, 0)),
            pl.BlockSpec((tm, CONV_CH_), lambda i: (i, 0)),
            full((1, N_HEADS * V_HEAD_)), full(w_att.shape), full(w_cnv.shape), full((1, D_MODEL_)),
        ],
        out_specs=[
            pl.BlockSpec((tm, D_MODEL_), lambda i: (i, 0)),
            pl.BlockSpec((tm, D_MODEL_), lambda i: (i, 0)),
        ],
        out_shape=[
            jax.ShapeDtypeStruct((t, D_MODEL_), F32),
            jax.ShapeDtypeStruct((t, D_MODEL_), BF16),
        ],
        compiler_params=pltpu.CompilerParams(
            dimension_semantics=("parallel",), vmem_limit_bytes=VMEM_LIMIT),
        name="outproj",
    )(x2, att, cnv, g_att, w_att, w_cnv, g_ffn)


N_TOP = PEER_TOPK_ + 1
TOP_ROWS = 24
ROWQ = 16
PAIR = 2
TILES_PER_TRIP = 4


def _sort_network(n):
    def merge(lo, hi, r):
        step = 2 * r
        if step < hi - lo:
            yield from merge(lo, hi, step)
            yield from merge(lo + r, hi, step)
            yield from ((i, i + r) for i in range(lo + r, hi - r, step))
        else:
            yield (lo, lo + r)

    def sort(lo, hi):
        if hi - lo >= 1:
            mid = lo + (hi - lo) // 2
            yield from sort(lo, mid)
            yield from sort(mid + 1, hi)
            yield from merge(lo, hi, 1)

    return list(sort(0, n - 1))


SORT16 = _sort_network(16)


def _extract_top(vals, n):
    tiles = [vals[8 * i:8 * (i + 1), :] for i in range(16)]
    for i, j in SORT16:
        tiles[i], tiles[j] = jnp.maximum(tiles[i], tiles[j]), jnp.minimum(tiles[i], tiles[j])
    out = []
    for k in range(n):
        m = jnp.max(tiles[0], axis=0, keepdims=True)
        out.append(m)
        hit = tiles[0] == m
        below = tiles[1:] + [jnp.full_like(m, -jnp.inf)]
        keep = min(n - k - 1, len(tiles))
        tiles = [jnp.where(hit, below[i], tiles[i]) for i in range(keep)]
    return out


def _peer_kernel(h_ref, xn_ref, wqT_ref, keys_ref, u_ref, vT_ref, o_ref,
                 s_sc, vtop_sc, e2_sc, thr_sc, a1_sc, at0_sc, at1_sc, c0_sc, c1_sc, outT_sc,
                 *, tb, ec):
    j = pl.program_id(1)
    ncol = tb // LANES
    n_i1 = ec // PEER_NKEYS_
    n_chunk = PEER_NKEYS_ // n_i1

    @pl.when(j == 0)
    def _scores_and_thresholds():
        qT = lax.dot_general(wqT_ref[...], xn_ref[...], NT_DIMS,
                             preferred_element_type=F32).astype(BF16)
        for hc in range(2 * PEER_HEADS_):
            sT = _dot(keys_ref[hc], qT[hc * PEER_DK_HALF_:(hc + 1) * PEER_DK_HALF_, :])
            for c in range(ncol):
                s_sc[hc, c] = sT[:, c * LANES:(c + 1) * LANES]
        vtop_sc[...] = jnp.full(vtop_sc.shape, -jnp.inf, F32)

        def thr_tile(idx):
            h = idx // ncol
            col = idx % ncol
            for c in range(2):
                for r, m in enumerate(_extract_top(s_sc[2 * h + c, col], N_TOP)):
                    vtop_sc[2 * h + c, col, r:r + 1, :] = m
            v1 = vtop_sc[2 * h, col]
            v2 = vtop_sc[2 * h + 1, col]
            blocks = [v1[0:1] + v2]
            blocks += [v1[a:a + 1] + v2[0:8] for a in range(1, 8)]
            blocks += [v1[8:TOP_ROWS] + v2[0:1]]
            blocks += [jnp.full((PEER_NKEYS_ - 4 * TOP_ROWS, LANES), -jnp.inf, F32)]
            cs = _extract_top(jnp.concatenate(blocks, axis=0), N_TOP)
            tau = 0.5 * (cs[PEER_TOPK_ - 1] + cs[PEER_TOPK_])
            z = jnp.exp(cs[0] - cs[0])
            for r in range(1, PEER_TOPK_):
                z = z + jnp.exp(cs[r] - cs[0])
            s1 = s_sc[2 * h, col]
            s2 = s_sc[2 * h + 1, col]
            e2_sc[h, col] = jnp.exp(s2 - v2[0:1])
            thr_sc[:, col, h] = (tau - s1).reshape(n_chunk, n_i1, LANES)
            a1_sc[:, col, h] = (jnp.exp(s1 - v1[0:1]) / z).reshape(n_chunk, n_i1, LANES)

        def thr_body(idx, carry):
            for sub in range(TILES_PER_TRIP):
                thr_tile(TILES_PER_TRIP * idx + sub)
            return carry

        lax.fori_loop(0, PEER_HEADS_ * ncol // TILES_PER_TRIP, thr_body, 0)
        outT_sc[...] = jnp.zeros(outT_sc.shape, F32)

    def zero_after(x):
        bits = lax.bitcast_convert_type(x, jnp.uint32)
        return lax.bitcast_convert_type((bits >> 16) >> 16, F32)

    def gate_column(at_src, c_dst, chunk, col, dep):
        for r0 in range(0, PEER_NKEYS_, ROWQ):
            zero = zero_after(dep)
            acc = [zero for _ in range(n_i1)]
            for h in range(PEER_HEADS_):
                s2 = s_sc[2 * h + 1, col, r0:r0 + ROWQ, :]
                e2 = e2_sc[h, col, r0:r0 + ROWQ, :]
                for i1l in range(n_i1):
                    thr = thr_sc[chunk, col, h, i1l:i1l + 1, :]
                    a1 = a1_sc[chunk, col, h, i1l:i1l + 1, :]
                    acc[i1l] = acc[i1l] + jnp.where(s2 >= thr, e2, 0.0) * a1
            for i1l in range(n_i1):
                rows = slice(i1l * PEER_NKEYS_ + r0, i1l * PEER_NKEYS_ + r0 + ROWQ)
                a = at_src[col, rows, :]
                gelu = 0.5 * a * (1.0 + lax.erf(a * np.float32(math.sqrt(0.5))))
                dep = gelu * acc[i1l]
                c_dst[rows, col * LANES:(col + 1) * LANES] = dep.astype(BF16)
        return dep

    hw = ec // 2
    dep = xn_ref[0:ROWQ, 0:LANES].astype(F32)
    for k, (at_sc, c_sc) in enumerate(((at0_sc, c0_sc), (at1_sc, c1_sc))):
        u_k = pltpu.bitcast(u_ref[k * hw:(k + 1) * hw, :], BF16)
        vt_k = pltpu.bitcast(vT_ref[:, k * ec:(k + 1) * ec], BF16)
        for n0 in range(0, tb, 2 * LANES):
            at = lax.dot_general(u_k, xn_ref[n0:n0 + 2 * LANES, :], NT_DIMS,
                                 preferred_element_type=F32)
            at_sc[n0 // LANES] = at[:, 0:LANES]
            at_sc[n0 // LANES + 1] = at[:, LANES:2 * LANES]
        for n0 in range(0, tb, 2 * LANES):
            for col in (n0 // LANES, n0 // LANES + 1):
                dep = gate_column(at_sc, c_sc, PAIR * j + k, col, dep)
            outT_sc[:, n0:n0 + 2 * LANES] += _dot(vt_k, c_sc[:, n0:n0 + 2 * LANES])

    @pl.when(j == pl.num_programs(1) - 1)
    def _fin():
        o_ref[...] = h_ref[...] + outT_sc[...].T


def _peer(h1, hn, wqT, keys, u_tab, vT_tab, tb, ec):
    t = h1.shape[0]
    n_exp = 2 * u_tab.shape[0]
    ncol = tb // LANES
    n_i1 = ec // PEER_NKEYS_
    n_pairs = n_exp // (PAIR * ec)
    return pl.pallas_call(
        functools.partial(_peer_kernel, tb=tb, ec=ec),
        grid=(t // tb, n_pairs),
        in_specs=[
            pl.BlockSpec((tb, D_MODEL_), lambda i, j: (i, 0)),
            pl.BlockSpec((tb, D_MODEL_), lambda i, j: (i, 0)),
            pl.BlockSpec(wqT.shape, lambda i, j: (0, 0)),
            pl.BlockSpec(keys.shape, lambda i, j: (0, 0, 0)),
            pl.BlockSpec((PAIR * ec // 2, D_MODEL_), lambda i, j: (j, 0)),
            pl.BlockSpec((D_MODEL_ // 2, PAIR * ec), lambda i, j: (0, j)),
        ],
        out_specs=pl.BlockSpec((tb, D_MODEL_), lambda i, j: (i, 0)),
        out_shape=jax.ShapeDtypeStruct((t, D_MODEL_), F32),
        scratch_shapes=[
            pltpu.VMEM((2 * PEER_HEADS_, ncol, PEER_NKEYS_, LANES), F32),
            pltpu.VMEM((2 * PEER_HEADS_, ncol, TOP_ROWS, LANES), F32),
            pltpu.VMEM((PEER_HEADS_, ncol, PEER_NKEYS_, LANES), F32),
            pltpu.VMEM((n_exp // ec, ncol, PEER_HEADS_, n_i1, LANES), F32),
            pltpu.VMEM((n_exp // ec, ncol, PEER_HEADS_, n_i1, LANES), F32),
            pltpu.VMEM((ncol, ec, LANES), F32),
            pltpu.VMEM((ncol, ec, LANES), F32),
            pltpu.VMEM((ec, tb), BF16),
            pltpu.VMEM((ec, tb), BF16),
            pltpu.VMEM((D_MODEL_, tb), F32),
        ],
        compiler_params=pltpu.CompilerParams(
            dimension_semantics=("parallel", "arbitrary"), vmem_limit_bytes=VMEM_LIMIT),
        name="peer",
    )(h1, hn, wqT, keys, u_tab, vT_tab)


def _ple_kernel(h_ref, p_ref, g_pl_ref, wg_ref, bg_ref, wp_ref, g_fin_ref, o_ref, *, last_layer):
    h = h_ref[...]
    gate = jax.nn.sigmoid(_dot(_rms(h, g_pl_ref[...]).astype(BF16), wg_ref[...]) + bg_ref[...])
    h = h + gate * _dot(p_ref[...].astype(BF16), wp_ref[...])
    o_ref[...] = _rms(h, g_fin_ref[...]) if last_layer else h


def _ple(h2, p2, g_pl, wg, bg, wp, g_fin, last_layer, tm):
    t = h2.shape[0]
    full = lambda shape: pl.BlockSpec(shape, lambda i: (0,) * len(shape))
    return pl.pallas_call(
        functools.partial(_ple_kernel, last_layer=last_layer),
        grid=(t // tm,),
        in_specs=[
            pl.BlockSpec((tm, D_MODEL_), lambda i: (i, 0)),
            pl.BlockSpec((tm, PLE_DIM_), lambda i: (i, 0)),
            full((1, D_MODEL_)), full(wg.shape), full((1, D_MODEL_)), full(wp.shape), full((1, D_MODEL_)),
        ],
        out_specs=pl.BlockSpec((tm, D_MODEL_), lambda i: (i, 0)),
        out_shape=jax.ShapeDtypeStruct((t, D_MODEL_), F32),
        compiler_params=pltpu.CompilerParams(
            dimension_semantics=("parallel",), vmem_limit_bytes=VMEM_LIMIT),
        name="ple",
    )(h2, p2, g_pl, wg, bg, wp, g_fin)


def _head_perm():
    pm = np.zeros((N_HEADS * (QK_NOPE_ + QK_ROPE_), N_HEADS * HEAD_PAD), np.float32)
    for h in range(N_HEADS):
        for d in range(QK_NOPE_):
            pm[h * QK_NOPE_ + d, h * HEAD_PAD + d] = 1.0
        for d in range(ROPE_HALF):
            pm[N_HEADS * QK_NOPE_ + h * ROPE_HALF + d, h * HEAD_PAD + QK_NOPE_ + d] = 1.0
            pm[N_HEADS * (QK_NOPE_ + ROPE_HALF) + h * ROPE_HALF + d,
               h * HEAD_PAD + QK_NOPE_ + ROPE_HALF + d] = 1.0
    return jnp.asarray(pm, BF16)


def _prep_w_in(w_in):
    q_lat = w_in[:, 0:Q_RANK_]
    kv_lat = w_in[:, Q_RANK_:Q_RANK_ + KV_RANK_]
    o = Q_RANK_ + KV_RANK_
    k1 = jnp.tile(w_in[:, o:o + ROPE_HALF], (1, N_HEADS))
    k2 = jnp.tile(w_in[:, o + ROPE_HALF:o + QK_ROPE_], (1, N_HEADS))
    conv = w_in[:, o + QK_ROPE_:]
    return jnp.concatenate([q_lat, kv_lat, k1, k2, conv], axis=1).astype(BF16)


def _prep_w_uq(w_uq):
    w = w_uq.reshape(Q_RANK_, N_HEADS, QK_NOPE_ + QK_ROPE_)
    nope = w[:, :, :QK_NOPE_].reshape(Q_RANK_, -1)
    r1 = w[:, :, QK_NOPE_:QK_NOPE_ + ROPE_HALF].reshape(Q_RANK_, -1)
    r2 = w[:, :, QK_NOPE_ + ROPE_HALF:].reshape(Q_RANK_, -1)
    return jnp.concatenate([nope, r1, r2], axis=1).astype(BF16)


def _prep_w_ukv(w_ukv):
    w = w_ukv.reshape(KV_RANK_, N_HEADS, QK_NOPE_ + V_HEAD_)
    k_nope = w[:, :, :QK_NOPE_].reshape(KV_RANK_, -1)
    v = w[:, :, QK_NOPE_:].reshape(KV_RANK_, -1)
    return jnp.concatenate([k_nope, v], axis=1).astype(BF16)


def _pack_u_kernel(x_ref, o_ref):
    o_ref[...] = pltpu.bitcast(x_ref[...].astype(BF16), jnp.uint32)


def _pack_vt_kernel(x_ref, o_ref):
    o_ref[...] = pltpu.bitcast(x_ref[...].T.astype(BF16), jnp.uint32)


def _pack_tables(u_tab, v_tab, rows):
    n, d = u_tab.shape
    params = pltpu.CompilerParams(dimension_semantics=("parallel",), vmem_limit_bytes=VMEM_LIMIT)
    u_pk = pl.pallas_call(
        _pack_u_kernel, grid=(n // rows,),
        in_specs=[pl.BlockSpec((rows, d), lambda i: (i, 0))],
        out_specs=pl.BlockSpec((rows // 2, d), lambda i: (i, 0)),
        out_shape=jax.ShapeDtypeStruct((n // 2, d), jnp.uint32),
        compiler_params=params, name="pack_u")(u_tab)
    vt_pk = pl.pallas_call(
        _pack_vt_kernel, grid=(n // rows,),
        in_specs=[pl.BlockSpec((rows, d), lambda i: (i, 0))],
        out_specs=pl.BlockSpec((d // 2, rows), lambda i: (0, i)),
        out_shape=jax.ShapeDtypeStruct((d // 2, n), jnp.uint32),
        compiler_params=params, name="pack_vt")(v_tab)
    return u_pk, vt_pk


def _tile(n, want):
    return want if n % want == 0 else n


def kernel(x, p, positions, attn_norm, w_in, q_norm, w_uq, kv_norm, w_ukv, conv_w, conv_b, conv_ln_g, conv_ln_b, attn_out_norm, conv_out_norm, w_out, ffn_norm, peer_wq, peer_keys, peer_u, peer_v, pl_norm, pl_gate_w, pl_gate_b, pl_proj, final_norm):
    b, s, d = x.shape
    t = b * s
    depth = w_in.shape[0]
    tm = _tile(t, 512)
    tseq = _tile(s, 512)
    row = lambda a: a.reshape(1, -1)

    h = x.reshape(t, d)
    posf = positions.astype(F32).reshape(t, 1)
    half_freqs = ROPE_THETA_ ** (-jnp.arange(ROPE_HALF, dtype=F32) / ROPE_HALF)
    freq = jnp.tile(half_freqs, N_HEADS).reshape(1, LANES)
    perm = _head_perm()

    for i in range(depth):
        q, k, v, glu = _inproj(h, posf, freq, row(attn_norm[i]), _prep_w_in(w_in[i]), row(q_norm[i]),
                               _prep_w_uq(w_uq[i]), row(kv_norm[i]), _prep_w_ukv(w_ukv[i]), perm, tm)
        att = _attention(q.reshape(b, s, -1), k.reshape(b, s, -1), v.reshape(b, s, -1), _tile(s, 1024))
        cnv = _conv_branch(glu.reshape(b, s, -1), conv_w[i], row(conv_b[i]), row(conv_ln_g[i]),
                           row(conv_ln_b[i]), row(conv_out_norm[i]), tseq)
        w_o = w_out[i].astype(BF16)
        h1, hn = _outproj(h, att.reshape(t, -1), cnv.reshape(t, -1), row(attn_out_norm[i]),
                          w_o[:N_HEADS * V_HEAD_], w_o[N_HEADS * V_HEAD_:], row(ffn_norm[i]), tm)
        keys = peer_keys[i].reshape(2 * PEER_HEADS_, PEER_NKEYS_, PEER_DK_HALF_).astype(BF16)
        u_pk, vt_pk = _pack_tables(peer_u[i], peer_v[i], 512)
        h2 = _peer(h1, hn, peer_wq[i].T.astype(BF16), keys, u_pk, vt_pk, tm, 1024)
        h = _ple(h2, p[i].reshape(t, -1), row(pl_norm[i]), pl_gate_w[i].astype(BF16),
                 row(pl_gate_b[i]), pl_proj[i].astype(BF16), row(final_norm), i == depth - 1, tm)
    return h.reshape(b, s, d)
```

```python
import functools
import math

import jax
import jax.numpy as jnp
import numpy as np
from jax import lax
from jax.experimental import pallas as pl
from jax.experimental.pallas import tpu as pltpu

F32 = jnp.float32
BF16 = jnp.bfloat16

D_MODEL_ = 1024
N_HEADS = 8
QK_NOPE_ = 64
QK_ROPE_ = 32
ROPE_HALF = QK_ROPE_ // 2
V_HEAD_ = 64
HEAD_PAD = 128
Q_RANK_ = 512
KV_RANK_ = 256
CONV_CH_ = 512
CONV_K_ = 31
CONV_HALO = 32
ROPE_THETA_ = 10000.0
PEER_HEADS_ = 8
PEER_NKEYS_ = 128
PEER_TOPK_ = 16
PEER_DK_HALF_ = 64
PLE_DIM_ = 256
EPS_ = 1e-6
NEG_BIG = -1e30
LANES = 128
SUBLANES = 8
VMEM_LIMIT = 56 * 1024 * 1024

NT_DIMS = (((1,), (1,)), ((), ()))
Q_PRESCALE = (QK_NOPE_ + QK_ROPE_) ** -0.5 * math.log2(math.e)


def _rms(x, g):
    return x * lax.rsqrt(jnp.mean(x * x, axis=-1, keepdims=True) + EPS_) * g


def _dot(a, b):
    return jnp.dot(a, b, preferred_element_type=F32)


def _inproj_kernel(x_ref, pos_ref, freq_ref, g_attn_ref, w_in_ref, g_q_ref, w_uq_ref,
                   g_kv_ref, w_ukv_ref, perm_ref, q_ref, k_ref, v_ref, glu_ref):
    x = x_ref[...]
    xn = _rms(x, g_attn_ref[...]).astype(BF16)
    z = _dot(xn, w_in_ref[...])
    q_lat = z[:, 0:512]
    kv_lat = z[:, 512:768]
    k1 = z[:, 768:896]
    k2 = z[:, 896:1024]
    glu_ref[...] = z[:, 1024:1536] * jax.nn.sigmoid(z[:, 1536:2048])

    q = _dot(_rms(q_lat, g_q_ref[...]).astype(BF16), w_uq_ref[...])
    kv = _dot(_rms(kv_lat, g_kv_ref[...]).astype(BF16), w_ukv_ref[...])
    ang = pos_ref[...] * freq_ref[...]
    cos = jnp.cos(ang)
    sin = jnp.sin(ang)
    p1 = q[:, 512:640]
    p2 = q[:, 640:768]
    qcat = jnp.concatenate([q[:, 0:512], p1 * cos - p2 * sin, p2 * cos + p1 * sin], axis=-1)
    qcat = qcat * np.float32(Q_PRESCALE)
    kcat = jnp.concatenate([kv[:, 0:512], k1 * cos - k2 * sin, k2 * cos + k1 * sin], axis=-1)
    q_ref[...] = _dot(qcat.astype(BF16), perm_ref[...]).astype(BF16)
    k_ref[...] = _dot(kcat.astype(BF16), perm_ref[...]).astype(BF16)
    v_ref[...] = kv[:, 512:1024].astype(BF16)


def _inproj(x2, posf, freq, g_attn, w_in_p, g_q, w_uq_p, g_kv, w_ukv_p, perm, tm):
    t = x2.shape[0]
    full = lambda shape: pl.BlockSpec(shape, lambda i: (0,) * len(shape))
    return pl.pallas_call(
        _inproj_kernel,
        grid=(t // tm,),
        in_specs=[
            pl.BlockSpec((tm, D_MODEL_), lambda i: (i, 0)),
            pl.BlockSpec((tm, 1), lambda i: (i, 0)),
            full((1, LANES)), full((1, D_MODEL_)), full(w_in_p.shape), full((1, Q_RANK_)),
            full(w_uq_p.shape), full((1, KV_RANK_)), full(w_ukv_p.shape), full(perm.shape),
        ],
        out_specs=[
            pl.BlockSpec((tm, N_HEADS * HEAD_PAD), lambda i: (i, 0)),
            pl.BlockSpec((tm, N_HEADS * HEAD_PAD), lambda i: (i, 0)),
            pl.BlockSpec((tm, N_HEADS * V_HEAD_), lambda i: (i, 0)),
            pl.BlockSpec((tm, CONV_CH_), lambda i: (i, 0)),
        ],
        out_shape=[
            jax.ShapeDtypeStruct((t, N_HEADS * HEAD_PAD), BF16),
            jax.ShapeDtypeStruct((t, N_HEADS * HEAD_PAD), BF16),
            jax.ShapeDtypeStruct((t, N_HEADS * V_HEAD_), BF16),
            jax.ShapeDtypeStruct((t, CONV_CH_), F32),
        ],
        compiler_params=pltpu.CompilerParams(
            dimension_semantics=("parallel",), vmem_limit_bytes=VMEM_LIMIT),
        name="inproj",
    )(x2, posf, freq, g_attn, w_in_p, g_q, w_uq_p, g_kv, w_ukv_p, perm)


def _attn_kernel(qi_ref, ki_ref, q_ref, k_ref, v_ref, o_ref, m_sc, l_sc, acc_sc, *, tq):
    t = pl.program_id(2)
    qi = qi_ref[t]
    ki = ki_ref[t]

    @pl.when(ki == 0)
    def _init():
        m_sc[...] = jnp.full(m_sc.shape, NEG_BIG, F32)
        l_sc[...] = jnp.zeros(l_sc.shape, F32)
        acc_sc[...] = jnp.zeros(acc_sc.shape, F32)

    def step(row0, n_rows, n_keys, diagonal):
        rows = slice(row0, row0 + n_rows)
        v = v_ref[0:n_keys, :]
        for hh in range(2):
            q = q_ref[rows, hh * HEAD_PAD:(hh + 1) * HEAD_PAD]
            k = k_ref[0:n_keys, hh * HEAD_PAD:(hh + 1) * HEAD_PAD]
            s = lax.dot_general(q, k, NT_DIMS, preferred_element_type=F32)
            if diagonal:
                row = row0 + lax.broadcasted_iota(jnp.int32, (n_rows, n_keys), 0)
                col = lax.broadcasted_iota(jnp.int32, (n_rows, n_keys), 1)
                s = jnp.where(col <= row, s, NEG_BIG)
            cols = [s[:, c * LANES:(c + 1) * LANES] for c in range(n_keys // LANES)]
            smax = functools.reduce(jnp.maximum, cols)
            m_prev = m_sc[hh, rows, :]
            m_new = jnp.maximum(m_prev, jnp.max(smax, axis=-1, keepdims=True))
            alpha = jnp.exp2(m_prev - m_new)
            p_cols = [jnp.exp2(c - m_new) for c in cols]
            l_sc[hh, rows, :] = alpha * l_sc[hh, rows, :] + functools.reduce(jnp.add, p_cols)
            p = jnp.concatenate(p_cols, axis=-1).astype(BF16)
            acc_sc[hh, rows, :] = alpha * acc_sc[hh, rows, :] + _dot(p, v)
            m_sc[hh, rows, :] = m_new

    @pl.when(ki < qi)
    def _full_tile():
        step(0, tq, tq, False)

    @pl.when(ki == qi)
    def _diagonal_tile():
        half = tq // 2
        step(0, half, half, True)
        step(half, half, tq, True)
        o0 = acc_sc[0] / jnp.sum(l_sc[0], axis=-1, keepdims=True)
        o1 = acc_sc[1] / jnp.sum(l_sc[1], axis=-1, keepdims=True)
        lane = lax.broadcasted_iota(jnp.int32, o0.shape, 1)
        o_ref[...] = jnp.where(lane < V_HEAD_, o0, o1)


def _attention(q, k, v, tq):
    b, s, _ = q.shape
    nq = s // tq
    qi_tab = np.asarray([qi for qi in range(nq) for _ in range(qi + 1)], np.int32)
    ki_tab = np.asarray([ki for qi in range(nq) for ki in range(qi + 1)], np.int32)
    grid_spec = pltpu.PrefetchScalarGridSpec(
        num_scalar_prefetch=2,
        grid=(b, N_HEADS // 2, len(qi_tab)),
        in_specs=[
            pl.BlockSpec((None, tq, 2 * HEAD_PAD), lambda b_, h, t, qt, kt: (b_, qt[t], h)),
            pl.BlockSpec((None, tq, 2 * HEAD_PAD), lambda b_, h, t, qt, kt: (b_, kt[t], h)),
            pl.BlockSpec((None, tq, 2 * V_HEAD_), lambda b_, h, t, qt, kt: (b_, kt[t], h)),
        ],
        out_specs=pl.BlockSpec((None, tq, 2 * V_HEAD_), lambda b_, h, t, qt, kt: (b_, qt[t], h)),
        scratch_shapes=[
            pltpu.VMEM((2, tq, LANES), F32),
            pltpu.VMEM((2, tq, LANES), F32),
            pltpu.VMEM((2, tq, 2 * V_HEAD_), F32),
        ],
    )
    return pl.pallas_call(
        functools.partial(_attn_kernel, tq=tq),
        grid_spec=grid_spec,
        out_shape=jax.ShapeDtypeStruct((b, s, N_HEADS * V_HEAD_), F32),
        compiler_params=pltpu.CompilerParams(
            dimension_semantics=("parallel", "parallel", "arbitrary"),
            vmem_limit_bytes=VMEM_LIMIT),
        name="attn",
    )(jnp.asarray(qi_tab), jnp.asarray(ki_tab), q, k, v)


def _conv_kernel(y_ref, halo_ref, w_ref, b_ref, lng_ref, lnb_ref, g_out_ref, o_ref, win_sc, *, ts, rc):
    i = pl.program_id(1)
    halo = halo_ref[...]
    win_sc[0, 0:CONV_HALO, :] = jnp.where(i == 0, jnp.zeros_like(halo), halo)
    win_sc[0, CONV_HALO:CONV_HALO + ts, :] = y_ref[...]
    span = CONV_HALO + ts - SUBLANES
    for r in range(1, SUBLANES):
        win_sc[r, 0:span, :] = win_sc[0, r:r + span, :]
    first = CONV_HALO - (CONV_K_ - 1)
    bias = b_ref[...]
    for r0 in range(0, ts, rc):
        acc = jnp.broadcast_to(bias, (rc, CONV_CH_))
        for kk in range(CONV_K_):
            a8, r = divmod(first + kk, SUBLANES)
            start = r0 + a8 * SUBLANES
            acc = acc + w_ref[kk:kk + 1, :] * win_sc[r, start:start + rc, :]
        mu = jnp.mean(acc, axis=-1, keepdims=True)
        cen = acc - mu
        var = jnp.mean(cen * cen, axis=-1, keepdims=True)
        yn = cen * lax.rsqrt(var + EPS_) * lng_ref[...] + lnb_ref[...]
        act = yn * jax.nn.sigmoid(yn)
        o_ref[r0:r0 + rc, :] = _rms(act, g_out_ref[...]).astype(BF16)


def _conv_branch(glu, conv_w, conv_b, ln_g, ln_b, g_out, ts):
    b, s, _ = glu.shape
    rc = 32
    full = lambda shape: pl.BlockSpec(shape, lambda b_, i: (0,) * len(shape))
    per_halo = ts // CONV_HALO
    return pl.pallas_call(
        functools.partial(_conv_kernel, ts=ts, rc=rc),
        grid=(b, s // ts),
        in_specs=[
            pl.BlockSpec((None, ts, CONV_CH_), lambda b_, i: (b_, i, 0)),
            pl.BlockSpec((None, CONV_HALO, CONV_CH_),
                         lambda b_, i: (b_, jnp.maximum(i * per_halo - 1, 0), 0)),
            full((CONV_K_, CONV_CH_)), full((1, CONV_CH_)), full((1, CONV_CH_)),
            full((1, CONV_CH_)), full((1, CONV_CH_)),
        ],
        out_specs=pl.BlockSpec((None, ts, CONV_CH_), lambda b_, i: (b_, i, 0)),
        out_shape=jax.ShapeDtypeStruct((b, s, CONV_CH_), BF16),
        scratch_shapes=[pltpu.VMEM((SUBLANES, CONV_HALO + ts, CONV_CH_), F32)],
        compiler_params=pltpu.CompilerParams(
            dimension_semantics=("parallel", "parallel"), vmem_limit_bytes=VMEM_LIMIT),
        name="conv",
    )(glu, glu, conv_w, conv_b, ln_g, ln_b, g_out)


def _outproj_kernel(x_ref, att_ref, cnv_ref, g_att_ref, w_att_ref, w_cnv_ref, g_ffn_ref,
                    h_ref, hn_ref):
    att_n = _rms(att_ref[...], g_att_ref[...]).astype(BF16)
    h = x_ref[...] + _dot(att_n, w_att_ref[...]) + _dot(cnv_ref[...], w_cnv_ref[...])
    h_ref[...] = h
    hn_ref[...] = _rms(h, g_ffn_ref[...]).astype(BF16)


def _outproj(x2, att, cnv, g_att, w_att, w_cnv, g_ffn, tm):
    t = x2.shape[0]
    full = lambda shape: pl.BlockSpec(shape, lambda i: (0,) * len(shape))
    return pl.pallas_call(
        _outproj_kernel,
        grid=(t // tm,),
        in_specs=[
            pl.BlockSpec((tm, D_MODEL_), lambda i: (i, 0)),
            pl.BlockSpec((tm, N_HEADS * V_HEAD_), lambda i: (i, 0)),
            pl.BlockSpec((tm, CONV_CH_), lambda i: (i, 0)),
            full((1, N_HEADS * V_HEAD_)), full(w_att.shape), full(w_cnv.shape), full((1, D_MODEL_)),
        ],
        out_specs=[
            pl.BlockSpec((tm, D_MODEL_), lambda i: (i, 0)),
            pl.BlockSpec((tm, D_MODEL_), lambda i: (i, 0)),
        ],
        out_shape=[
            jax.ShapeDtypeStruct((t, D_MODEL_), F32),
            jax.ShapeDtypeStruct((t, D_MODEL_), BF16),
        ],
        compiler_params=pltpu.CompilerParams(
            dimension_semantics=("parallel",), vmem_limit_bytes=VMEM_LIMIT),
        name="outproj",
    )(x2, att, cnv, g_att, w_att, w_cnv, g_ffn)


N_TOP = PEER_TOPK_ + 1
TOP_ROWS = 24
ROWQ = 16
PAIR = 2
TILES_PER_TRIP = 8


def _sort_network(n):
    def merge(lo, hi, r):
        step = 2 * r
        if step < hi - lo:
            yield from merge(lo, hi, step)
            yield from merge(lo + r, hi, step)
            yield from ((i, i + r) for i in range(lo + r, hi - r, step))
        else:
            yield (lo, lo + r)

    def sort(lo, hi):
        if hi - lo >= 1:
            mid = lo + (hi - lo) // 2
            yield from sort(lo, mid)
            yield from sort(mid + 1, hi)
            yield from merge(lo, hi, 1)

    return list(sort(0, n - 1))


SORT16 = _sort_network(16)


def _extract_top(vals, n):
    tiles = [vals[8 * i:8 * (i + 1), :] for i in range(16)]
    for i, j in SORT16:
        tiles[i], tiles[j] = jnp.maximum(tiles[i], tiles[j]), jnp.minimum(tiles[i], tiles[j])
    out = []
    for k in range(n):
        m = jnp.max(tiles[0], axis=0, keepdims=True)
        out.append(m)
        hit = tiles[0] == m
        below = tiles[1:] + [jnp.full_like(m, -jnp.inf)]
        keep = min(n - k - 1, len(tiles))
        tiles = [jnp.where(hit, below[i], tiles[i]) for i in range(keep)]
    return out


def _peer_kernel(h_ref, xn_ref, wqT_ref, keys_ref, u_ref, vT_ref, o_ref,
                 s_sc, vtop_sc, e2_sc, thr_sc, a1_sc, at0_sc, at1_sc, c0_sc, c1_sc, outT_sc,
                 *, tb, ec):
    j = pl.program_id(1)
    ncol = tb // LANES
    n_i1 = ec // PEER_NKEYS_
    n_chunk = PEER_NKEYS_ // n_i1

    @pl.when(j == 0)
    def _scores_and_thresholds():
        qT = lax.dot_general(wqT_ref[...], xn_ref[...], NT_DIMS,
                             preferred_element_type=F32).astype(BF16)
        for hc in range(2 * PEER_HEADS_):
            sT = _dot(keys_ref[hc], qT[hc * PEER_DK_HALF_:(hc + 1) * PEER_DK_HALF_, :])
            for c in range(ncol):
                s_sc[hc, c] = sT[:, c * LANES:(c + 1) * LANES]
        vtop_sc[...] = jnp.full(vtop_sc.shape, -jnp.inf, F32)

        def thr_tile(idx):
            h = idx // ncol
            col = idx % ncol
            for c in range(2):
                for r, m in enumerate(_extract_top(s_sc[2 * h + c, col], N_TOP)):
                    vtop_sc[2 * h + c, col, r:r + 1, :] = m
            v1 = vtop_sc[2 * h, col]
            v2 = vtop_sc[2 * h + 1, col]
            blocks = [v1[0:1] + v2]
            blocks += [v1[a:a + 1] + v2[0:8] for a in range(1, 8)]
            blocks += [v1[8:TOP_ROWS] + v2[0:1]]
            blocks += [jnp.full((PEER_NKEYS_ - 4 * TOP_ROWS, LANES), -jnp.inf, F32)]
            cs = _extract_top(jnp.concatenate(blocks, axis=0), N_TOP)
            tau = 0.5 * (cs[PEER_TOPK_ - 1] + cs[PEER_TOPK_])
            z = jnp.exp(cs[0] - cs[0])
            for r in range(1, PEER_TOPK_):
                z = z + jnp.exp(cs[r] - cs[0])
            s1 = s_sc[2 * h, col]
            s2 = s_sc[2 * h + 1, col]
            e2_sc[h, col] = jnp.exp(s2 - v2[0:1])
            thr_sc[:, col, h] = (tau - s1).reshape(n_chunk, n_i1, LANES)
            a1_sc[:, col, h] = (jnp.exp(s1 - v1[0:1]) / z).reshape(n_chunk, n_i1, LANES)

        def thr_body(idx, carry):
            for sub in range(TILES_PER_TRIP):
                thr_tile(TILES_PER_TRIP * idx + sub)
            return carry

        lax.fori_loop(0, PEER_HEADS_ * ncol // TILES_PER_TRIP, thr_body, 0)
        outT_sc[...] = jnp.zeros(outT_sc.shape, F32)

    def zero_after(x):
        bits = lax.bitcast_convert_type(x, jnp.uint32)
        return lax.bitcast_convert_type((bits >> 16) >> 16, F32)

    def gate_column(at_src, c_dst, chunk, col, dep):
        for r0 in range(0, PEER_NKEYS_, ROWQ):
            zero = zero_after(dep)
            acc = [zero for _ in range(n_i1)]
            for h in range(PEER_HEADS_):
                s2 = s_sc[2 * h + 1, col, r0:r0 + ROWQ, :]
                e2 = e2_sc[h, col, r0:r0 + ROWQ, :]
                for i1l in range(n_i1):
                    thr = thr_sc[chunk, col, h, i1l:i1l + 1, :]
                    a1 = a1_sc[chunk, col, h, i1l:i1l + 1, :]
                    acc[i1l] = acc[i1l] + jnp.where(s2 >= thr, e2, 0.0) * a1
            for i1l in range(n_i1):
                rows = slice(i1l * PEER_NKEYS_ + r0, i1l * PEER_NKEYS_ + r0 + ROWQ)
                a = at_src[col, rows, :]
                gelu = 0.5 * a * (1.0 + lax.erf(a * np.float32(math.sqrt(0.5))))
                dep = gelu * acc[i1l]
                c_dst[rows, col * LANES:(col + 1) * LANES] = dep.astype(BF16)
        return dep

    hw = ec // 2
    dep = xn_ref[0:ROWQ, 0:LANES].astype(F32)
    for k, (at_sc, c_sc) in enumerate(((at0_sc, c0_sc), (at1_sc, c1_sc))):
        u_k = pltpu.bitcast(u_ref[k * hw:(k + 1) * hw, :], BF16)
        vt_k = pltpu.bitcast(vT_ref[:, k * ec:(k + 1) * ec], BF16)
        for n0 in range(0, tb, 2 * LANES):
            at = lax.dot_general(u_k, xn_ref[n0:n0 + 2 * LANES, :], NT_DIMS,
                                 preferred_element_type=F32)
            at_sc[n0 // LANES] = at[:, 0:LANES]
            at_sc[n0 // LANES + 1] = at[:, LANES:2 * LANES]
        for n0 in range(0, tb, 2 * LANES):
            for col in (n0 // LANES, n0 // LANES + 1):
                dep = gate_column(at_sc, c_sc, PAIR * j + k, col, dep)
            outT_sc[:, n0:n0 + 2 * LANES] += _dot(vt_k, c_sc[:, n0:n0 + 2 * LANES])

    @pl.when(j == pl.num_programs(1) - 1)
    def _fin():
        o_ref[...] = h_ref[...] + outT_sc[...].T


def _peer(h1, hn, wqT, keys, u_tab, vT_tab, tb, ec):
    t = h1.shape[0]
    n_exp = 2 * u_tab.shape[0]
    ncol = tb // LANES
    n_i1 = ec // PEER_NKEYS_
    n_pairs = n_exp // (PAIR * ec)
    return pl.pallas_call(
        functools.partial(_peer_kernel, tb=tb, ec=ec),
        grid=(t // tb, n_pairs),
        in_specs=[
            pl.BlockSpec((tb, D_MODEL_), lambda i, j: (i, 0)),
            pl.BlockSpec((tb, D_MODEL_), lambda i, j: (i, 0)),
            pl.BlockSpec(wqT.shape, lambda i, j: (0, 0)),
            pl.BlockSpec(keys.shape, lambda i, j: (0, 0, 0)),
            pl.BlockSpec((PAIR * ec // 2, D_MODEL_), lambda i, j: (j, 0)),
            pl.BlockSpec((D_MODEL_ // 2, PAIR * ec), lambda i, j: (0, j)),
        ],
        out_specs=pl.BlockSpec((tb, D_MODEL_), lambda i, j: (i, 0)),
        out_shape=jax.ShapeDtypeStruct((t, D_MODEL_), F32),
        scratch_shapes=[
            pltpu.VMEM((2 * PEER_HEADS_, ncol, PEER_NKEYS_, LANES), F32),
            pltpu.VMEM((2 * PEER_HEADS_, ncol, TOP_ROWS, LANES), F32),
            pltpu.VMEM((PEER_HEADS_, ncol, PEER_NKEYS_, LANES), F32),
            pltpu.VMEM((n_exp // ec, ncol, PEER_HEADS_, n_i1, LANES), F32),
            pltpu.VMEM((n_exp // ec, ncol, PEER_HEADS_, n_i1, LANES), F32),
            pltpu.VMEM((ncol, ec, LANES), F32),
            pltpu.VMEM((ncol, ec, LANES), F32),
            pltpu.VMEM((ec, tb), BF16),
            pltpu.VMEM((ec, tb), BF16),
            pltpu.VMEM((D_MODEL_, tb), F32),
        ],
        compiler_params=pltpu.CompilerParams(
            dimension_semantics=("parallel", "arbitrary"), vmem_limit_bytes=VMEM_LIMIT),
        name="peer",
    )(h1, hn, wqT, keys, u_tab, vT_tab)


def _ple_kernel(h_ref, p_ref, g_pl_ref, wg_ref, bg_ref, wp_ref, g_fin_ref, o_ref, *, last_layer):
    h = h_ref[...]
    gate = jax.nn.sigmoid(_dot(_rms(h, g_pl_ref[...]).astype(BF16), wg_ref[...]) + bg_ref[...])
    h = h + gate * _dot(p_ref[...].astype(BF16), wp_ref[...])
    o_ref[...] = _rms(h, g_fin_ref[...]) if last_layer else h


def _ple(h2, p2, g_pl, wg, bg, wp, g_fin, last_layer, tm):
    t = h2.shape[0]
    full = lambda shape: pl.BlockSpec(shape, lambda i: (0,) * len(shape))
    return pl.pallas_call(
        functools.partial(_ple_kernel, last_layer=last_layer),
        grid=(t // tm,),
        in_specs=[
            pl.BlockSpec((tm, D_MODEL_), lambda i: (i, 0)),
            pl.BlockSpec((tm, PLE_DIM_), lambda i: (i, 0)),
            full((1, D_MODEL_)), full(wg.shape), full((1, D_MODEL_)), full(wp.shape), full((1, D_MODEL_)),
        ],
        out_specs=pl.BlockSpec((tm, D_MODEL_), lambda i: (i, 0)),
        out_shape=jax.ShapeDtypeStruct((t, D_MODEL_), F32),
        compiler_params=pltpu.CompilerParams(
            dimension_semantics=("parallel",), vmem_limit_bytes=VMEM_LIMIT),
        name="ple",
    )(h2, p2, g_pl, wg, bg, wp, g_fin)


def _head_perm():
    pm = np.zeros((N_HEADS * (QK_NOPE_ + QK_ROPE_), N_HEADS * HEAD_PAD), np.float32)
    for h in range(N_HEADS):
        for d in range(QK_NOPE_):
            pm[h * QK_NOPE_ + d, h * HEAD_PAD + d] = 1.0
        for d in range(ROPE_HALF):
            pm[N_HEADS * QK_NOPE_ + h * ROPE_HALF + d, h * HEAD_PAD + QK_NOPE_ + d] = 1.0
            pm[N_HEADS * (QK_NOPE_ + ROPE_HALF) + h * ROPE_HALF + d,
               h * HEAD_PAD + QK_NOPE_ + ROPE_HALF + d] = 1.0
    return jnp.asarray(pm, BF16)


def _prep_w_in(w_in):
    q_lat = w_in[:, 0:Q_RANK_]
    kv_lat = w_in[:, Q_RANK_:Q_RANK_ + KV_RANK_]
    o = Q_RANK_ + KV_RANK_
    k1 = jnp.tile(w_in[:, o:o + ROPE_HALF], (1, N_HEADS))
    k2 = jnp.tile(w_in[:, o + ROPE_HALF:o + QK_ROPE_], (1, N_HEADS))
    conv = w_in[:, o + QK_ROPE_:]
    return jnp.concatenate([q_lat, kv_lat, k1, k2, conv], axis=1).astype(BF16)


def _prep_w_uq(w_uq):
    w = w_uq.reshape(Q_RANK_, N_HEADS, QK_NOPE_ + QK_ROPE_)
    nope = w[:, :, :QK_NOPE_].reshape(Q_RANK_, -1)
    r1 = w[:, :, QK_NOPE_:QK_NOPE_ + ROPE_HALF].reshape(Q_RANK_, -1)
    r2 = w[:, :, QK_NOPE_ + ROPE_HALF:].reshape(Q_RANK_, -1)
    return jnp.concatenate([nope, r1, r2], axis=1).astype(BF16)


def _prep_w_ukv(w_ukv):
    w = w_ukv.reshape(KV_RANK_, N_HEADS, QK_NOPE_ + V_HEAD_)
    k_nope = w[:, :, :QK_NOPE_].reshape(KV_RANK_, -1)
    v = w[:, :, QK_NOPE_:].reshape(KV_RANK_, -1)
    return jnp.concatenate([k_nope, v], axis=1).astype(BF16)


def _pack_u_kernel(x_ref, o_ref):
    o_ref[...] = pltpu.bitcast(x_ref[...].astype(BF16), jnp.uint32)


def _pack_vt_kernel(x_ref, o_ref):
    o_ref[...] = pltpu.bitcast(x_ref[...].T.astype(BF16), jnp.uint32)


def _pack_tables(u_tab, v_tab, rows):
    n, d = u_tab.shape
    params = pltpu.CompilerParams(dimension_semantics=("parallel",), vmem_limit_bytes=VMEM_LIMIT)
    u_pk = pl.pallas_call(
        _pack_u_kernel, grid=(n // rows,),
        in_specs=[pl.BlockSpec((rows, d), lambda i: (i, 0))],
        out_specs=pl.BlockSpec((rows // 2, d), lambda i: (i, 0)),
        out_shape=jax.ShapeDtypeStruct((n // 2, d), jnp.uint32),
        compiler_params=params, name="pack_u")(u_tab)
    vt_pk = pl.pallas_call(
        _pack_vt_kernel, grid=(n // rows,),
        in_specs=[pl.BlockSpec((rows, d), lambda i: (i, 0))],
        out_specs=pl.BlockSpec((d // 2, rows), lambda i: (0, i)),
        out_shape=jax.ShapeDtypeStruct((d // 2, n), jnp.uint32),
        compiler_params=params, name="pack_vt")(v_tab)
    return u_pk, vt_pk


def _tile(n, want):
    return want if n % want == 0 else n


def kernel(x, p, positions, attn_norm, w_in, q_norm, w_uq, kv_norm, w_ukv, conv_w, conv_b, conv_ln_g, conv_ln_b, attn_out_norm, conv_out_norm, w_out, ffn_norm, peer_wq, peer_keys, peer_u, peer_v, pl_norm, pl_gate_w, pl_gate_b, pl_proj, final_norm):
    b, s, d = x.shape
    t = b * s
    depth = w_in.shape[0]
    tm = _tile(t, 512)
    tseq = _tile(s, 512)
    row = lambda a: a.reshape(1, -1)

    h = x.reshape(t, d)
    posf = positions.astype(F32).reshape(t, 1)
    half_freqs = ROPE_THETA_ ** (-jnp.arange(ROPE_HALF, dtype=F32) / ROPE_HALF)
    freq = jnp.tile(half_freqs, N_HEADS).reshape(1, LANES)
    perm = _head_perm()

    for i in range(depth):
        q, k, v, glu = _inproj(h, posf, freq, row(attn_norm[i]), _prep_w_in(w_in[i]), row(q_norm[i]),
                               _prep_w_uq(w_uq[i]), row(kv_norm[i]), _prep_w_ukv(w_ukv[i]), perm, tm)
        att = _attention(q.reshape(b, s, -1), k.reshape(b, s, -1), v.reshape(b, s, -1), _tile(s, 1024))
        cnv = _conv_branch(glu.reshape(b, s, -1), conv_w[i], row(conv_b[i]), row(conv_ln_g[i]),
                           row(conv_ln_b[i]), row(conv_out_norm[i]), tseq)
        w_o = w_out[i].astype(BF16)
        h1, hn = _outproj(h, att.reshape(t, -1), cnv.reshape(t, -1), row(attn_out_norm[i]),
                          w_o[:N_HEADS * V_HEAD_], w_o[N_HEADS * V_HEAD_:], row(ffn_norm[i]), tm)
        keys = peer_keys[i].reshape(2 * PEER_HEADS_, PEER_NKEYS_, PEER_DK_HALF_).astype(BF16)
        u_pk, vt_pk = _pack_tables(peer_u[i], peer_v[i], 512)
        h2 = _peer(h1, hn, peer_wq[i].T.astype(BF16), keys, u_pk, vt_pk, tm, 1024)
        h = _ple(h2, p[i].reshape(t, -1), row(pl_norm[i]), pl_gate_w[i].astype(BF16),
                 row(pl_gate_b[i]), pl_proj[i].astype(BF16), row(final_norm), i == depth - 1, tm)
    return h.reshape(b, s, d)
```

```python
import functools
import math

import jax
import jax.numpy as jnp
import numpy as np
from jax import lax
from jax.experimental import pallas as pl
from jax.experimental.pallas import tpu as pltpu

F32 = jnp.float32
BF16 = jnp.bfloat16

D_MODEL_ = 1024
N_HEADS = 8
QK_NOPE_ = 64
QK_ROPE_ = 32
ROPE_HALF = QK_ROPE_ // 2
V_HEAD_ = 64
HEAD_PAD = 128
Q_RANK_ = 512
KV_RANK_ = 256
CONV_CH_ = 512
CONV_K_ = 31
CONV_HALO = 32
ROPE_THETA_ = 10000.0
PEER_HEADS_ = 8
PEER_NKEYS_ = 128
PEER_TOPK_ = 16
PEER_DK_HALF_ = 64
PLE_DIM_ = 256
EPS_ = 1e-6
NEG_BIG = -1e30
LANES = 128
SUBLANES = 8
VMEM_LIMIT = 56 * 1024 * 1024

NT_DIMS = (((1,), (1,)), ((), ()))
Q_PRESCALE = (QK_NOPE_ + QK_ROPE_) ** -0.5 * math.log2(math.e)


def _rms(x, g):
    return x * lax.rsqrt(jnp.mean(x * x, axis=-1, keepdims=True) + EPS_) * g


def _dot(a, b):
    return jnp.dot(a, b, preferred_element_type=F32)


def _inproj_kernel(x_ref, pos_ref, freq_ref, g_attn_ref, w_in_ref, g_q_ref, wq_nope_ref, wq_rope_ref,
                   g_kv_ref, wk_nope_ref, wv_ref, perm_ref, q_ref, k_ref, v_ref, glu_ref):
    x = x_ref[...]
    xn = _rms(x, g_attn_ref[...]).astype(BF16)
    z = _dot(xn, w_in_ref[...])
    q_lat = z[:, 0:512]
    kv_lat = z[:, 512:768]
    k1 = z[:, 768:896]
    k2 = z[:, 896:1024]
    glu_ref[...] = z[:, 1024:1536] * jax.nn.sigmoid(z[:, 1536:2048])

    qn = _rms(q_lat, g_q_ref[...]).astype(BF16)
    kvn = _rms(kv_lat, g_kv_ref[...]).astype(BF16)
    q_rope = _dot(qn, wq_rope_ref[...])
    ang = pos_ref[...] * freq_ref[...]
    cos = jnp.cos(ang)
    sin = jnp.sin(ang)
    p1 = q_rope[:, 0:LANES]
    p2 = q_rope[:, LANES:2 * LANES]
    scale = np.float32(Q_PRESCALE)
    q_rot = (jnp.concatenate([p1 * cos - p2 * sin, p2 * cos + p1 * sin], axis=-1) * scale).astype(BF16)
    k_rot = jnp.concatenate([k1 * cos - k2 * sin, k2 * cos + k1 * sin], axis=-1).astype(BF16)
    q_ref[...] = (_dot(qn, wq_nope_ref[...]) * scale + _dot(q_rot, perm_ref[...])).astype(BF16)
    k_ref[...] = (_dot(kvn, wk_nope_ref[...]) + _dot(k_rot, perm_ref[...])).astype(BF16)
    v_ref[...] = _dot(kvn, wv_ref[...]).astype(BF16)


def _inproj(x2, posf, freq, g_attn, w_in_p, g_q, wq_nope, wq_rope, g_kv, wk_nope, wv, perm, tm):
    t = x2.shape[0]
    full = lambda shape: pl.BlockSpec(shape, lambda i: (0,) * len(shape))
    return pl.pallas_call(
        _inproj_kernel,
        grid=(t // tm,),
        in_specs=[
            pl.BlockSpec((tm, D_MODEL_), lambda i: (i, 0)),
            pl.BlockSpec((tm, 1), lambda i: (i, 0)),
            full((1, LANES)), full((1, D_MODEL_)), full(w_in_p.shape), full((1, Q_RANK_)),
            full(wq_nope.shape), full(wq_rope.shape), full((1, KV_RANK_)), full(wk_nope.shape),
            full(wv.shape), full(perm.shape),
        ],
        out_specs=[
            pl.BlockSpec((tm, N_HEADS * HEAD_PAD), lambda i: (i, 0)),
            pl.BlockSpec((tm, N_HEADS * HEAD_PAD), lambda i: (i, 0)),
            pl.BlockSpec((tm, N_HEADS * V_HEAD_), lambda i: (i, 0)),
            pl.BlockSpec((tm, CONV_CH_), lambda i: (i, 0)),
        ],
        out_shape=[
            jax.ShapeDtypeStruct((t, N_HEADS * HEAD_PAD), BF16),
            jax.ShapeDtypeStruct((t, N_HEADS * HEAD_PAD), BF16),
            jax.ShapeDtypeStruct((t, N_HEADS * V_HEAD_), BF16),
            jax.ShapeDtypeStruct((t, CONV_CH_), F32),
        ],
        compiler_params=pltpu.CompilerParams(
            dimension_semantics=("parallel",), vmem_limit_bytes=VMEM_LIMIT),
        name="inproj",
    )(x2, posf, freq, g_attn, w_in_p, g_q, wq_nope, wq_rope, g_kv, wk_nope, wv, perm)


def _attn_kernel(qi_ref, ki_ref, q_ref, k_ref, v_ref, o_ref, m_sc, l_sc, acc_sc, *, tq):
    t = pl.program_id(2)
    qi = qi_ref[t]
    ki = ki_ref[t]

    @pl.when(ki == 0)
    def _init():
        m_sc[...] = jnp.full(m_sc.shape, NEG_BIG, F32)
        l_sc[...] = jnp.zeros(l_sc.shape, F32)
        acc_sc[...] = jnp.zeros(acc_sc.shape, F32)

    def step(row0, n_rows, n_keys, diagonal):
        rows = slice(row0, row0 + n_rows)
        v = v_ref[0:n_keys, :]
        for hh in range(2):
            q = q_ref[rows, hh * HEAD_PAD:(hh + 1) * HEAD_PAD]
            k = k_ref[0:n_keys, hh * HEAD_PAD:(hh + 1) * HEAD_PAD]
            s = lax.dot_general(q, k, NT_DIMS, preferred_element_type=F32)
            if diagonal:
                row = row0 + lax.broadcasted_iota(jnp.int32, (n_rows, n_keys), 0)
                col = lax.broadcasted_iota(jnp.int32, (n_rows, n_keys), 1)
                s = jnp.where(col <= row, s, NEG_BIG)
            cols = [s[:, c * LANES:(c + 1) * LANES] for c in range(n_keys // LANES)]
            smax = functools.reduce(jnp.maximum, cols)
            m_prev = m_sc[hh, rows, :]
            m_new = jnp.maximum(m_prev, jnp.max(smax, axis=-1, keepdims=True))
            alpha = jnp.exp2(m_prev - m_new)
            p_cols = [jnp.exp2(c - m_new) for c in cols]
            l_sc[hh, rows, :] = alpha * l_sc[hh, rows, :] + functools.reduce(jnp.add, p_cols)
            p = jnp.concatenate(p_cols, axis=-1).astype(BF16)
            acc_sc[hh, rows, :] = alpha * acc_sc[hh, rows, :] + _dot(p, v)
            m_sc[hh, rows, :] = m_new

    @pl.when(ki < qi)
    def _full_tile():
        step(0, tq, tq, False)

    @pl.when(ki == qi)
    def _diagonal_tile():
        half = tq // 2
        step(0, half, half, True)
        step(half, half, tq, True)
        o0 = acc_sc[0] / jnp.sum(l_sc[0], axis=-1, keepdims=True)
        o1 = acc_sc[1] / jnp.sum(l_sc[1], axis=-1, keepdims=True)
        lane = lax.broadcasted_iota(jnp.int32, o0.shape, 1)
        o_ref[...] = jnp.where(lane < V_HEAD_, o0, o1)


def _attention(q, k, v, tq):
    b, s, _ = q.shape
    nq = s // tq
    qi_tab = np.asarray([qi for qi in range(nq) for _ in range(qi + 1)], np.int32)
    ki_tab = np.asarray([ki for qi in range(nq) for ki in range(qi + 1)], np.int32)
    grid_spec = pltpu.PrefetchScalarGridSpec(
        num_scalar_prefetch=2,
        grid=(b, N_HEADS // 2, len(qi_tab)),
        in_specs=[
            pl.BlockSpec((None, tq, 2 * HEAD_PAD), lambda b_, h, t, qt, kt: (b_, qt[t], h)),
            pl.BlockSpec((None, tq, 2 * HEAD_PAD), lambda b_, h, t, qt, kt: (b_, kt[t], h)),
            pl.BlockSpec((None, tq, 2 * V_HEAD_), lambda b_, h, t, qt, kt: (b_, kt[t], h)),
        ],
        out_specs=pl.BlockSpec((None, tq, 2 * V_HEAD_), lambda b_, h, t, qt, kt: (b_, qt[t], h)),
        scratch_shapes=[
            pltpu.VMEM((2, tq, LANES), F32),
            pltpu.VMEM((2, tq, LANES), F32),
            pltpu.VMEM((2, tq, 2 * V_HEAD_), F32),
        ],
    )
    return pl.pallas_call(
        functools.partial(_attn_kernel, tq=tq),
        grid_spec=grid_spec,
        out_shape=jax.ShapeDtypeStruct((b, s, N_HEADS * V_HEAD_), F32),
        compiler_params=pltpu.CompilerParams(
            dimension_semantics=("parallel", "parallel", "arbitrary"),
            vmem_limit_bytes=VMEM_LIMIT),
        name="attn",
    )(jnp.asarray(qi_tab), jnp.asarray(ki_tab), q, k, v)


def _conv_kernel(y_ref, halo_ref, w_ref, b_ref, lng_ref, lnb_ref, g_out_ref, o_ref, win_sc, *, ts, rc):
    i = pl.program_id(1)
    halo = halo_ref[...]
    win_sc[0, 0:CONV_HALO, :] = jnp.where(i == 0, jnp.zeros_like(halo), halo)
    win_sc[0, CONV_HALO:CONV_HALO + ts, :] = y_ref[...]
    span = CONV_HALO + ts - SUBLANES
    for r in range(1, SUBLANES):
        win_sc[r, 0:span, :] = win_sc[0, r:r + span, :]
    first = CONV_HALO - (CONV_K_ - 1)
    bias = b_ref[...]
    for r0 in range(0, ts, rc):
        acc = jnp.broadcast_to(bias, (rc, CONV_CH_))
        for kk in range(CONV_K_):
            a8, r = divmod(first + kk, SUBLANES)
            start = r0 + a8 * SUBLANES
            acc = acc + w_ref[kk:kk + 1, :] * win_sc[r, start:start + rc, :]
        mu = jnp.mean(acc, axis=-1, keepdims=True)
        cen = acc - mu
        var = jnp.mean(cen * cen, axis=-1, keepdims=True)
        yn = cen * lax.rsqrt(var + EPS_) * lng_ref[...] + lnb_ref[...]
        act = yn * jax.nn.sigmoid(yn)
        o_ref[r0:r0 + rc, :] = _rms(act, g_out_ref[...]).astype(BF16)


def _conv_branch(glu, conv_w, conv_b, ln_g, ln_b, g_out, ts):
    b, s, _ = glu.shape
    rc = 32
    full = lambda shape: pl.BlockSpec(shape, lambda b_, i: (0,) * len(shape))
    per_halo = ts // CONV_HALO
    return pl.pallas_call(
        functools.partial(_conv_kernel, ts=ts, rc=rc),
        grid=(b, s // ts),
        in_specs=[
            pl.BlockSpec((None, ts, CONV_CH_), lambda b_, i: (b_, i, 0)),
            pl.BlockSpec((None, CONV_HALO, CONV_CH_),
                         lambda b_, i: (b_, jnp.maximum(i * per_halo - 1, 0), 0)),
            full((CONV_K_, CONV_CH_)), full((1, CONV_CH_)), full((1, CONV_CH_)),
            full((1, CONV_CH_)), full((1, CONV_CH_)),
        ],
        out_specs=pl.BlockSpec((None, ts, CONV_CH_), lambda b_, i: (b_, i, 0)),
        out_shape=jax.ShapeDtypeStruct((b, s, CONV_CH_), BF16),
        scratch_shapes=[pltpu.VMEM((SUBLANES, CONV_HALO + ts, CONV_CH_), F32)],
        compiler_params=pltpu.CompilerParams(
            dimension_semantics=("parallel", "parallel"), vmem_limit_bytes=VMEM_LIMIT),
        name="conv",
    )(glu, glu, conv_w, conv_b, ln_g, ln_b, g_out)


def _outproj_kernel(x_ref, att_ref, cnv_ref, g_att_ref, w_att_ref, w_cnv_ref, g_ffn_ref,
                    h_ref, hn_ref):
    att_n = _rms(att_ref[...], g_att_ref[...]).astype(BF16)
    h = x_ref[...] + _dot(att_n, w_att_ref[...]) + _dot(cnv_ref[...], w_cnv_ref[...])
    h_ref[...] = h
    hn_ref[...] = _rms(h, g_ffn_ref[...]).astype(BF16)


def _outproj(x2, att, cnv, g_att, w_att, w_cnv, g_ffn, tm):
    t = x2.shape[0]
    full = lambda shape: pl.BlockSpec(shape, lambda i: (0,) * len(shape))
    return pl.pallas_call(
        _outproj_kernel,
        grid=(t // tm,),
        in_specs=[
            pl.BlockSpec((tm, D_MODEL_), lambda i: (i, 0)),
            pl.BlockSpec((tm, N_HEADS * V_HEAD_), lambda i: (i, 0)),
            pl.BlockSpec((tm, CONV_CH_), lambda i: (i, 0)),
            full((1, N_HEADS * V_HEAD_)), full(w_att.shape), full(w_cnv.shape), full((1, D_MODEL_)),
        ],
        out_specs=[
            pl.BlockSpec((tm, D_MODEL_), lambda i: (i, 0)),
            pl.BlockSpec((tm, D_MODEL_), lambda i: (i, 0)),
        ],
        out_shape=[
            jax.ShapeDtypeStruct((t, D_MODEL_), F32),
            jax.ShapeDtypeStruct((t, D_MODEL_), BF16),
        ],
        compiler_params=pltpu.CompilerParams(
            dimension_semantics=("parallel",), vmem_limit_bytes=VMEM_LIMIT),
        name="outproj",
    )(x2, att, cnv, g_att, w_att, w_cnv, g_ffn)


N_TOP = PEER_TOPK_ + 1
TOP_ROWS = 24
ROWQ = 16
PAIR = 2
TILES_PER_TRIP = 8


def _sort_network(n):
    def merge(lo, hi, r):
        step = 2 * r
        if step < hi - lo:
            yield from merge(lo, hi, step)
            yield from merge(lo + r, hi, step)
            yield from ((i, i + r) for i in range(lo + r, hi - r, step))
        else:
            yield (lo, lo + r)

    def sort(lo, hi):
        if hi - lo >= 1:
            mid = lo + (hi - lo) // 2
            yield from sort(lo, mid)
            yield from sort(mid + 1, hi)
            yield from merge(lo, hi, 1)

    return list(sort(0, n - 1))


SORT16 = _sort_network(16)


def _extract_top(vals, n):
    tiles = [vals[8 * i:8 * (i + 1), :] for i in range(16)]
    for i, j in SORT16:
        tiles[i], tiles[j] = jnp.maximum(tiles[i], tiles[j]), jnp.minimum(tiles[i], tiles[j])
    out = []
    for k in range(n):
        m = jnp.max(tiles[0], axis=0, keepdims=True)
        out.append(m)
        hit = tiles[0] == m
        below = tiles[1:] + [jnp.full_like(m, -jnp.inf)]
        keep = min(n - k - 1, len(tiles))
        tiles = [jnp.where(hit, below[i], tiles[i]) for i in range(keep)]
    return out


def _peer_kernel(h_ref, xn_ref, wqT_ref, keys_ref, u_ref, vT_ref, o_ref,
                 s_sc, vtop_sc, e2_sc, thr_sc, a1_sc, at0_sc, at1_sc, c0_sc, c1_sc, outT_sc,
                 *, tb, ec):
    j = pl.program_id(1)
    ncol = tb // LANES
    n_i1 = ec // PEER_NKEYS_
    n_chunk = PEER_NKEYS_ // n_i1

    @pl.when(j == 0)
    def _scores_and_thresholds():
        qT = lax.dot_general(wqT_ref[...], xn_ref[...], NT_DIMS,
                             preferred_element_type=F32).astype(BF16)
        for hc in range(2 * PEER_HEADS_):
            sT = _dot(keys_ref[hc], qT[hc * PEER_DK_HALF_:(hc + 1) * PEER_DK_HALF_, :])
            for c in range(ncol):
                s_sc[hc, c] = sT[:, c * LANES:(c + 1) * LANES]
        vtop_sc[...] = jnp.full(vtop_sc.shape, -jnp.inf, F32)

        def thr_tile(idx):
            h = idx // ncol
            col = idx % ncol
            for c in range(2):
                for r, m in enumerate(_extract_top(s_sc[2 * h + c, col], N_TOP)):
                    vtop_sc[2 * h + c, col, r:r + 1, :] = m
            v1 = vtop_sc[2 * h, col]
            v2 = vtop_sc[2 * h + 1, col]
            blocks = [v1[0:1] + v2]
            blocks += [v1[a:a + 1] + v2[0:8] for a in range(1, 8)]
            blocks += [v1[8:TOP_ROWS] + v2[0:1]]
            blocks += [jnp.full((PEER_NKEYS_ - 4 * TOP_ROWS, LANES), -jnp.inf, F32)]
            cs = _extract_top(jnp.concatenate(blocks, axis=0), N_TOP)
            tau = 0.5 * (cs[PEER_TOPK_ - 1] + cs[PEER_TOPK_])
            z = jnp.exp(cs[0] - cs[0])
            for r in range(1, PEER_TOPK_):
                z = z + jnp.exp(cs[r] - cs[0])
            s1 = s_sc[2 * h, col]
            s2 = s_sc[2 * h + 1, col]
            e2_sc[h, col] = jnp.exp(s2 - v2[0:1])
            thr_sc[:, col, h] = (tau - s1).reshape(n_chunk, n_i1, LANES)
            a1_sc[:, col, h] = (0.5 * jnp.exp(s1 - v1[0:1]) / z).reshape(n_chunk, n_i1, LANES)

        def thr_body(idx, carry):
            for sub in range(TILES_PER_TRIP):
                thr_tile(TILES_PER_TRIP * idx + sub)
            return carry

        lax.fori_loop(0, PEER_HEADS_ * ncol // TILES_PER_TRIP, thr_body, 0)
        outT_sc[...] = jnp.zeros(outT_sc.shape, F32)

    def zero_after(x):
        bits = lax.bitcast_convert_type(x, jnp.uint32)
        return lax.bitcast_convert_type((bits >> 16) >> 16, F32)

    def gate_column(at_src, c_dst, chunk, col, dep):
        for r0 in range(0, PEER_NKEYS_, ROWQ):
            zero = zero_after(dep)
            acc = [zero for _ in range(n_i1)]
            for h in range(PEER_HEADS_):
                s2 = s_sc[2 * h + 1, col, r0:r0 + ROWQ, :]
                e2 = e2_sc[h, col, r0:r0 + ROWQ, :]
                for i1l in range(n_i1):
                    thr = thr_sc[chunk, col, h, i1l:i1l + 1, :]
                    a1 = a1_sc[chunk, col, h, i1l:i1l + 1, :]
                    acc[i1l] = acc[i1l] + jnp.where(s2 >= thr, e2, 0.0) * a1
            for i1l in range(n_i1):
                rows = slice(i1l * PEER_NKEYS_ + r0, i1l * PEER_NKEYS_ + r0 + ROWQ)
                a = at_src[col, rows, :]
                dep = a * (1.0 + lax.erf(a * np.float32(math.sqrt(0.5)))) * acc[i1l]
                c_dst[rows, col * LANES:(col + 1) * LANES] = dep.astype(BF16)
        return dep

    hw = ec // 2
    dep = xn_ref[0:ROWQ, 0:LANES].astype(F32)
    for k, (at_sc, c_sc) in enumerate(((at0_sc, c0_sc), (at1_sc, c1_sc))):
        u_k = pltpu.bitcast(u_ref[k * hw:(k + 1) * hw, :], BF16)
        vt_k = pltpu.bitcast(vT_ref[:, k * ec:(k + 1) * ec], BF16)
        for n0 in range(0, tb, 2 * LANES):
            at = lax.dot_general(u_k, xn_ref[n0:n0 + 2 * LANES, :], NT_DIMS,
                                 preferred_element_type=F32)
            at_sc[n0 // LANES] = at[:, 0:LANES]
            at_sc[n0 // LANES + 1] = at[:, LANES:2 * LANES]
        for n0 in range(0, tb, 2 * LANES):
            for col in (n0 // LANES, n0 // LANES + 1):
                dep = gate_column(at_sc, c_sc, PAIR * j + k, col, dep)
            outT_sc[:, n0:n0 + 2 * LANES] += _dot(vt_k, c_sc[:, n0:n0 + 2 * LANES])

    @pl.when(j == pl.num_programs(1) - 1)
    def _fin():
        o_ref[...] = h_ref[...] + outT_sc[...].T


def _peer(h1, hn, wqT, keys, u_tab, vT_tab, tb, ec):
    t = h1.shape[0]
    n_exp = 2 * u_tab.shape[0]
    ncol = tb // LANES
    n_i1 = ec // PEER_NKEYS_
    n_pairs = n_exp // (PAIR * ec)
    return pl.pallas_call(
        functools.partial(_peer_kernel, tb=tb, ec=ec),
        grid=(t // tb, n_pairs),
        in_specs=[
            pl.BlockSpec((tb, D_MODEL_), lambda i, j: (i, 0)),
            pl.BlockSpec((tb, D_MODEL_), lambda i, j: (i, 0)),
            pl.BlockSpec(wqT.shape, lambda i, j: (0, 0)),
            pl.BlockSpec(keys.shape, lambda i, j: (0, 0, 0)),
            pl.BlockSpec((PAIR * ec // 2, D_MODEL_), lambda i, j: (j, 0)),
            pl.BlockSpec((D_MODEL_ // 2, PAIR * ec), lambda i, j: (0, j)),
        ],
        out_specs=pl.BlockSpec((tb, D_MODEL_), lambda i, j: (i, 0)),
        out_shape=jax.ShapeDtypeStruct((t, D_MODEL_), F32),
        scratch_shapes=[
            pltpu.VMEM((2 * PEER_HEADS_, ncol, PEER_NKEYS_, LANES), F32),
            pltpu.VMEM((2 * PEER_HEADS_, ncol, TOP_ROWS, LANES), F32),
            pltpu.VMEM((PEER_HEADS_, ncol, PEER_NKEYS_, LANES), F32),
            pltpu.VMEM((n_exp // ec, ncol, PEER_HEADS_, n_i1, LANES), F32),
            pltpu.VMEM((n_exp // ec, ncol, PEER_HEADS_, n_i1, LANES), F32),
            pltpu.VMEM((ncol, ec, LANES), F32),
            pltpu.VMEM((ncol, ec, LANES), F32),
            pltpu.VMEM((ec, tb), BF16),
            pltpu.VMEM((ec, tb), BF16),
            pltpu.VMEM((D_MODEL_, tb), F32),
        ],
        compiler_params=pltpu.CompilerParams(
            dimension_semantics=("parallel", "arbitrary"), vmem_limit_bytes=VMEM_LIMIT),
        name="peer",
    )(h1, hn, wqT, keys, u_tab, vT_tab)


def _ple_kernel(h_ref, p_ref, g_pl_ref, wg_ref, bg_ref, wp_ref, g_fin_ref, o_ref, *, last_layer):
    h = h_ref[...]
    gate = jax.nn.sigmoid(_dot(_rms(h, g_pl_ref[...]).astype(BF16), wg_ref[...]) + bg_ref[...])
    h = h + gate * _dot(p_ref[...].astype(BF16), wp_ref[...])
    o_ref[...] = _rms(h, g_fin_ref[...]) if last_layer else h


def _ple(h2, p2, g_pl, wg, bg, wp, g_fin, last_layer, tm):
    t = h2.shape[0]
    full = lambda shape: pl.BlockSpec(shape, lambda i: (0,) * len(shape))
    return pl.pallas_call(
        functools.partial(_ple_kernel, last_layer=last_layer),
        grid=(t // tm,),
        in_specs=[
            pl.BlockSpec((tm, D_MODEL_), lambda i: (i, 0)),
            pl.BlockSpec((tm, PLE_DIM_), lambda i: (i, 0)),
            full((1, D_MODEL_)), full(wg.shape), full((1, D_MODEL_)), full(wp.shape), full((1, D_MODEL_)),
        ],
        out_specs=pl.BlockSpec((tm, D_MODEL_), lambda i: (i, 0)),
        out_shape=jax.ShapeDtypeStruct((t, D_MODEL_), F32),
        compiler_params=pltpu.CompilerParams(
            dimension_semantics=("parallel",), vmem_limit_bytes=VMEM_LIMIT),
        name="ple",
    )(h2, p2, g_pl, wg, bg, wp, g_fin)


def _rope_perm():
    pm = np.zeros((N_HEADS * QK_ROPE_, N_HEADS * HEAD_PAD), np.float32)
    for h in range(N_HEADS):
        for d in range(ROPE_HALF):
            pm[h * ROPE_HALF + d, h * HEAD_PAD + QK_NOPE_ + d] = 1.0
            pm[N_HEADS * ROPE_HALF + h * ROPE_HALF + d, h * HEAD_PAD + QK_NOPE_ + ROPE_HALF + d] = 1.0
    return jnp.asarray(pm, BF16)


def _pad_heads(w):
    return jnp.pad(w, ((0, 0), (0, 0), (0, HEAD_PAD - QK_NOPE_))).reshape(w.shape[0], -1)


def _prep_w_in(w_in):
    q_lat = w_in[:, 0:Q_RANK_]
    kv_lat = w_in[:, Q_RANK_:Q_RANK_ + KV_RANK_]
    o = Q_RANK_ + KV_RANK_
    k1 = jnp.tile(w_in[:, o:o + ROPE_HALF], (1, N_HEADS))
    k2 = jnp.tile(w_in[:, o + ROPE_HALF:o + QK_ROPE_], (1, N_HEADS))
    conv = w_in[:, o + QK_ROPE_:]
    return jnp.concatenate([q_lat, kv_lat, k1, k2, conv], axis=1).astype(BF16)


def _prep_w_uq(w_uq):
    w = w_uq.reshape(Q_RANK_, N_HEADS, QK_NOPE_ + QK_ROPE_)
    r1 = w[:, :, QK_NOPE_:QK_NOPE_ + ROPE_HALF].reshape(Q_RANK_, -1)
    r2 = w[:, :, QK_NOPE_ + ROPE_HALF:].reshape(Q_RANK_, -1)
    return _pad_heads(w[:, :, :QK_NOPE_]).astype(BF16), jnp.concatenate([r1, r2], axis=1).astype(BF16)


def _prep_w_ukv(w_ukv):
    w = w_ukv.reshape(KV_RANK_, N_HEADS, QK_NOPE_ + V_HEAD_)
    return _pad_heads(w[:, :, :QK_NOPE_]).astype(BF16), w[:, :, QK_NOPE_:].reshape(KV_RANK_, -1).astype(BF16)


def _pack_u_kernel(x_ref, o_ref):
    o_ref[...] = pltpu.bitcast(x_ref[...].astype(BF16), jnp.uint32)


def _pack_vt_kernel(x_ref, o_ref):
    o_ref[...] = pltpu.bitcast(x_ref[...].T.astype(BF16), jnp.uint32)


def _pack_tables(u_tab, v_tab, rows):
    n, d = u_tab.shape
    params = pltpu.CompilerParams(dimension_semantics=("parallel",), vmem_limit_bytes=VMEM_LIMIT)
    u_pk = pl.pallas_call(
        _pack_u_kernel, grid=(n // rows,),
        in_specs=[pl.BlockSpec((rows, d), lambda i: (i, 0))],
        out_specs=pl.BlockSpec((rows // 2, d), lambda i: (i, 0)),
        out_shape=jax.ShapeDtypeStruct((n // 2, d), jnp.uint32),
        compiler_params=params, name="pack_u")(u_tab)
    vt_pk = pl.pallas_call(
        _pack_vt_kernel, grid=(n // rows,),
        in_specs=[pl.BlockSpec((rows, d), lambda i: (i, 0))],
        out_specs=pl.BlockSpec((d // 2, rows), lambda i: (0, i)),
        out_shape=jax.ShapeDtypeStruct((d // 2, n), jnp.uint32),
        compiler_params=params, name="pack_vt")(v_tab)
    return u_pk, vt_pk


def _tile(n, want):
    return want if n % want == 0 else n


def kernel(x, p, positions, attn_norm, w_in, q_norm, w_uq, kv_norm, w_ukv, conv_w, conv_b, conv_ln_g, conv_ln_b, attn_out_norm, conv_out_norm, w_out, ffn_norm, peer_wq, peer_keys, peer_u, peer_v, pl_norm, pl_gate_w, pl_gate_b, pl_proj, final_norm):
    b, s, d = x.shape
    t = b * s
    depth = w_in.shape[0]
    tm = _tile(t, 512)
    tseq = _tile(s, 512)
    row = lambda a: a.reshape(1, -1)

    h = x.reshape(t, d)
    posf = positions.astype(F32).reshape(t, 1)
    half_freqs = ROPE_THETA_ ** (-jnp.arange(ROPE_HALF, dtype=F32) / ROPE_HALF)
    freq = jnp.tile(half_freqs, N_HEADS).reshape(1, LANES)
    perm = _rope_perm()

    for i in range(depth):
        q, k, v, glu = _inproj(h, posf, freq, row(attn_norm[i]), _prep_w_in(w_in[i]), row(q_norm[i]),
                               *_prep_w_uq(w_uq[i]), row(kv_norm[i]), *_prep_w_ukv(w_ukv[i]), perm, tm)
        att = _attention(q.reshape(b, s, -1), k.reshape(b, s, -1), v.reshape(b, s, -1), _tile(s, 1024))
        cnv = _conv_branch(glu.reshape(b, s, -1), conv_w[i], row(conv_b[i]), row(conv_ln_g[i]),
                           row(conv_ln_b[i]), row(conv_out_norm[i]), tseq)
        w_o = w_out[i].astype(BF16)
        h1, hn = _outproj(h, att.reshape(t, -1), cnv.reshape(t, -1), row(attn_out_norm[i]),
                          w_o[:N_HEADS * V_HEAD_], w_o[N_HEADS * V_HEAD_:], row(ffn_norm[i]), tm)
        keys = peer_keys[i].reshape(2 * PEER_HEADS_, PEER_NKEYS_, PEER_DK_HALF_).astype(BF16)
        u_pk, vt_pk = _pack_tables(peer_u[i], peer_v[i], 512)
        h2 = _peer(h1, hn, peer_wq[i].T.astype(BF16), keys, u_pk, vt_pk, tm, 1024)
        h = _ple(h2, p[i].reshape(t, -1), row(pl_norm[i]), pl_gate_w[i].astype(BF16),
                 row(pl_gate_b[i]), pl_proj[i].astype(BF16), row(final_norm), i == depth - 1, tm)
    return h.reshape(b, s, d)
```

```python
import functools
import math

import jax
import jax.numpy as jnp
import numpy as np
from jax import lax
from jax.experimental import pallas as pl
from jax.experimental.pallas import tpu as pltpu

F32 = jnp.float32
BF16 = jnp.bfloat16

D_MODEL_ = 1024
N_HEADS = 8
QK_NOPE_ = 64
QK_ROPE_ = 32
ROPE_HALF = QK_ROPE_ // 2
V_HEAD_ = 64
HEAD_PAD = 128
Q_RANK_ = 512
KV_RANK_ = 256
CONV_CH_ = 512
CONV_K_ = 31
CONV_HALO = 32
ROPE_THETA_ = 10000.0
PEER_HEADS_ = 8
PEER_NKEYS_ = 128
PEER_TOPK_ = 16
PEER_DK_HALF_ = 64
PLE_DIM_ = 256
EPS_ = 1e-6
NEG_BIG = -1e30
LANES = 128
SUBLANES = 8
VMEM_LIMIT = 56 * 1024 * 1024

NT_DIMS = (((1,), (1,)), ((), ()))
Q_PRESCALE = (QK_NOPE_ + QK_ROPE_) ** -0.5 * math.log2(math.e)


def _rms(x, g):
    return x * lax.rsqrt(jnp.mean(x * x, axis=-1, keepdims=True) + EPS_) * g


def _dot(a, b):
    return jnp.dot(a, b, preferred_element_type=F32)


def _inproj_kernel(x_ref, pos_ref, freq_ref, g_attn_ref, w_in_ref, g_q_ref, wq_nope_ref, wq_rope_ref,
                   g_kv_ref, wk_nope_ref, wv_ref, perm_ref, q_ref, k_ref, v_ref, glu_ref):
    x = x_ref[...]
    xn = _rms(x, g_attn_ref[...]).astype(BF16)
    z = _dot(xn, w_in_ref[...])
    q_lat = z[:, 0:512]
    kv_lat = z[:, 512:768]
    k1 = z[:, 768:896]
    k2 = z[:, 896:1024]
    glu_ref[...] = z[:, 1024:1536] * jax.nn.sigmoid(z[:, 1536:2048])

    qn = _rms(q_lat, g_q_ref[...]).astype(BF16)
    kvn = _rms(kv_lat, g_kv_ref[...]).astype(BF16)
    q_rope = _dot(qn, wq_rope_ref[...])
    ang = pos_ref[...] * freq_ref[...]
    cos = jnp.cos(ang)
    sin = jnp.sin(ang)
    p1 = q_rope[:, 0:LANES]
    p2 = q_rope[:, LANES:2 * LANES]
    scale = np.float32(Q_PRESCALE)
    q_rot = (jnp.concatenate([p1 * cos - p2 * sin, p2 * cos + p1 * sin], axis=-1) * scale).astype(BF16)
    k_rot = jnp.concatenate([k1 * cos - k2 * sin, k2 * cos + k1 * sin], axis=-1).astype(BF16)
    q_ref[...] = (_dot(qn, wq_nope_ref[...]) * scale + _dot(q_rot, perm_ref[...])).astype(BF16)
    k_ref[...] = (_dot(kvn, wk_nope_ref[...]) + _dot(k_rot, perm_ref[...])).astype(BF16)
    v_ref[...] = _dot(kvn, wv_ref[...]).astype(BF16)


def _inproj(x2, posf, freq, g_attn, w_in_p, g_q, wq_nope, wq_rope, g_kv, wk_nope, wv, perm, tm):
    t = x2.shape[0]
    full = lambda shape: pl.BlockSpec(shape, lambda i: (0,) * len(shape))
    return pl.pallas_call(
        _inproj_kernel,
        grid=(t // tm,),
        in_specs=[
            pl.BlockSpec((tm, D_MODEL_), lambda i: (i, 0)),
            pl.BlockSpec((tm, 1), lambda i: (i, 0)),
            full((1, LANES)), full((1, D_MODEL_)), full(w_in_p.shape), full((1, Q_RANK_)),
            full(wq_nope.shape), full(wq_rope.shape), full((1, KV_RANK_)), full(wk_nope.shape),
            full(wv.shape), full(perm.shape),
        ],
        out_specs=[
            pl.BlockSpec((tm, N_HEADS * HEAD_PAD), lambda i: (i, 0)),
            pl.BlockSpec((tm, N_HEADS * HEAD_PAD), lambda i: (i, 0)),
            pl.BlockSpec((tm, N_HEADS * V_HEAD_), lambda i: (i, 0)),
            pl.BlockSpec((tm, CONV_CH_), lambda i: (i, 0)),
        ],
        out_shape=[
            jax.ShapeDtypeStruct((t, N_HEADS * HEAD_PAD), BF16),
            jax.ShapeDtypeStruct((t, N_HEADS * HEAD_PAD), BF16),
            jax.ShapeDtypeStruct((t, N_HEADS * V_HEAD_), BF16),
            jax.ShapeDtypeStruct((t, CONV_CH_), F32),
        ],
        compiler_params=pltpu.CompilerParams(
            dimension_semantics=("parallel",), vmem_limit_bytes=VMEM_LIMIT),
        name="inproj",
    )(x2, posf, freq, g_attn, w_in_p, g_q, wq_nope, wq_rope, g_kv, wk_nope, wv, perm)


def _attn_kernel(qi_ref, ki_ref, q_ref, k_ref, v_ref, o_ref, m_sc, l_sc, acc_sc, *, tq):
    t = pl.program_id(2)
    qi = qi_ref[t]
    ki = ki_ref[t]

    @pl.when(ki == 0)
    def _init():
        m_sc[...] = jnp.full(m_sc.shape, NEG_BIG, F32)
        l_sc[...] = jnp.zeros(l_sc.shape, F32)
        acc_sc[...] = jnp.zeros(acc_sc.shape, F32)

    def step(row0, n_rows, n_keys, diagonal):
        rows = slice(row0, row0 + n_rows)
        v = v_ref[0:n_keys, :]
        for hh in range(2):
            q = q_ref[rows, hh * HEAD_PAD:(hh + 1) * HEAD_PAD]
            k = k_ref[0:n_keys, hh * HEAD_PAD:(hh + 1) * HEAD_PAD]
            s = lax.dot_general(q, k, NT_DIMS, preferred_element_type=F32)
            if diagonal:
                row = row0 + lax.broadcasted_iota(jnp.int32, (n_rows, n_keys), 0)
                col = lax.broadcasted_iota(jnp.int32, (n_rows, n_keys), 1)
                s = jnp.where(col <= row, s, NEG_BIG)
            cols = [s[:, c * LANES:(c + 1) * LANES] for c in range(n_keys // LANES)]
            smax = functools.reduce(jnp.maximum, cols)
            m_prev = m_sc[hh, rows, :]
            m_new = jnp.maximum(m_prev, jnp.max(smax, axis=-1, keepdims=True))
            alpha = jnp.exp2(m_prev - m_new)
            p_cols = [jnp.exp2(c - m_new) for c in cols]
            l_sc[hh, rows, :] = alpha * l_sc[hh, rows, :] + functools.reduce(jnp.add, p_cols)
            p = jnp.concatenate(p_cols, axis=-1).astype(BF16)
            acc_sc[hh, rows, :] = alpha * acc_sc[hh, rows, :] + _dot(p, v)
            m_sc[hh, rows, :] = m_new

    @pl.when(ki < qi)
    def _full_tile():
        step(0, tq, tq, False)

    @pl.when(ki == qi)
    def _diagonal_tile():
        half = tq // 2
        step(0, half, half, True)
        step(half, half, tq, True)
        o0 = acc_sc[0] / jnp.sum(l_sc[0], axis=-1, keepdims=True)
        o1 = acc_sc[1] / jnp.sum(l_sc[1], axis=-1, keepdims=True)
        lane = lax.broadcasted_iota(jnp.int32, o0.shape, 1)
        o_ref[...] = jnp.where(lane < V_HEAD_, o0, o1)


def _attention(q, k, v, tq):
    b, s, _ = q.shape
    nq = s // tq
    qi_tab = np.asarray([qi for qi in range(nq) for _ in range(qi + 1)], np.int32)
    ki_tab = np.asarray([ki for qi in range(nq) for ki in range(qi + 1)], np.int32)
    grid_spec = pltpu.PrefetchScalarGridSpec(
        num_scalar_prefetch=2,
        grid=(b, N_HEADS // 2, len(qi_tab)),
        in_specs=[
            pl.BlockSpec((None, tq, 2 * HEAD_PAD), lambda b_, h, t, qt, kt: (b_, qt[t], h)),
            pl.BlockSpec((None, tq, 2 * HEAD_PAD), lambda b_, h, t, qt, kt: (b_, kt[t], h)),
            pl.BlockSpec((None, tq, 2 * V_HEAD_), lambda b_, h, t, qt, kt: (b_, kt[t], h)),
        ],
        out_specs=pl.BlockSpec((None, tq, 2 * V_HEAD_), lambda b_, h, t, qt, kt: (b_, qt[t], h)),
        scratch_shapes=[
            pltpu.VMEM((2, tq, LANES), F32),
            pltpu.VMEM((2, tq, LANES), F32),
            pltpu.VMEM((2, tq, 2 * V_HEAD_), F32),
        ],
    )
    return pl.pallas_call(
        functools.partial(_attn_kernel, tq=tq),
        grid_spec=grid_spec,
        out_shape=jax.ShapeDtypeStruct((b, s, N_HEADS * V_HEAD_), F32),
        compiler_params=pltpu.CompilerParams(
            dimension_semantics=("parallel", "parallel", "arbitrary"),
            vmem_limit_bytes=VMEM_LIMIT),
        name="attn",
    )(jnp.asarray(qi_tab), jnp.asarray(ki_tab), q, k, v)


def _conv_kernel(y_ref, halo_ref, w_ref, b_ref, lng_ref, lnb_ref, g_out_ref, o_ref, win_sc, *, ts, rc):
    i = pl.program_id(1)
    halo = halo_ref[...]
    win_sc[0, 0:CONV_HALO, :] = jnp.where(i == 0, jnp.zeros_like(halo), halo)
    win_sc[0, CONV_HALO:CONV_HALO + ts, :] = y_ref[...]
    span = CONV_HALO + ts - SUBLANES
    for r in range(1, SUBLANES):
        win_sc[r, 0:span, :] = win_sc[0, r:r + span, :]
    first = CONV_HALO - (CONV_K_ - 1)
    bias = b_ref[...]
    for r0 in range(0, ts, rc):
        acc = jnp.broadcast_to(bias, (rc, CONV_CH_))
        for kk in range(CONV_K_):
            a8, r = divmod(first + kk, SUBLANES)
            start = r0 + a8 * SUBLANES
            acc = acc + w_ref[kk:kk + 1, :] * win_sc[r, start:start + rc, :]
        mu = jnp.mean(acc, axis=-1, keepdims=True)
        cen = acc - mu
        var = jnp.mean(cen * cen, axis=-1, keepdims=True)
        yn = cen * lax.rsqrt(var + EPS_) * lng_ref[...] + lnb_ref[...]
        act = yn * jax.nn.sigmoid(yn)
        o_ref[r0:r0 + rc, :] = _rms(act, g_out_ref[...]).astype(BF16)


def _conv_branch(glu, conv_w, conv_b, ln_g, ln_b, g_out, ts):
    b, s, _ = glu.shape
    rc = 32
    full = lambda shape: pl.BlockSpec(shape, lambda b_, i: (0,) * len(shape))
    per_halo = ts // CONV_HALO
    return pl.pallas_call(
        functools.partial(_conv_kernel, ts=ts, rc=rc),
        grid=(b, s // ts),
        in_specs=[
            pl.BlockSpec((None, ts, CONV_CH_), lambda b_, i: (b_, i, 0)),
            pl.BlockSpec((None, CONV_HALO, CONV_CH_),
                         lambda b_, i: (b_, jnp.maximum(i * per_halo - 1, 0), 0)),
            full((CONV_K_, CONV_CH_)), full((1, CONV_CH_)), full((1, CONV_CH_)),
            full((1, CONV_CH_)), full((1, CONV_CH_)),
        ],
        out_specs=pl.BlockSpec((None, ts, CONV_CH_), lambda b_, i: (b_, i, 0)),
        out_shape=jax.ShapeDtypeStruct((b, s, CONV_CH_), BF16),
        scratch_shapes=[pltpu.VMEM((SUBLANES, CONV_HALO + ts, CONV_CH_), F32)],
        compiler_params=pltpu.CompilerParams(
            dimension_semantics=("parallel", "parallel"), vmem_limit_bytes=VMEM_LIMIT),
        name="conv",
    )(glu, glu, conv_w, conv_b, ln_g, ln_b, g_out)


def _outproj_kernel(x_ref, att_ref, cnv_ref, g_att_ref, w_att_ref, w_cnv_ref, g_ffn_ref,
                    h_ref, hn_ref):
    att_n = _rms(att_ref[...], g_att_ref[...]).astype(BF16)
    h = x_ref[...] + _dot(att_n, w_att_ref[...]) + _dot(cnv_ref[...], w_cnv_ref[...])
    h_ref[...] = h
    hn_ref[...] = _rms(h, g_ffn_ref[...]).astype(BF16)


def _outproj(x2, att, cnv, g_att, w_att, w_cnv, g_ffn, tm):
    t = x2.shape[0]
    full = lambda shape: pl.BlockSpec(shape, lambda i: (0,) * len(shape))
    return pl.pallas_call(
        _outproj_kernel,
        grid=(t // tm,),
        in_specs=[
            pl.BlockSpec((tm, D_MODEL_), lambda i: (i, 0)),
            pl.BlockSpec((tm, N_HEADS * V_HEAD_), lambda i: (i, 0)),
            pl.BlockSpec((tm, CONV_CH_), lambda i: (i, 0)),
            full((1, N_HEADS * V_HEAD_)), full(w_att.shape), full(w_cnv.shape), full((1, D_MODEL_)),
        ],
        out_specs=[
            pl.BlockSpec((tm, D_MODEL_), lambda i: (i, 0)),
            pl.BlockSpec((tm, D_MODEL_), lambda i: (i, 0)),
        ],
        out_shape=[
            jax.ShapeDtypeStruct((t, D_MODEL_), F32),
            jax.ShapeDtypeStruct((t, D_MODEL_), BF16),
        ],
        compiler_params=pltpu.CompilerParams(
            dimension_semantics=("parallel",), vmem_limit_bytes=VMEM_LIMIT),
        name="outproj",
    )(x2, att, cnv, g_att, w_att, w_cnv, g_ffn)


N_TOP = PEER_TOPK_ + 1
TOP_ROWS = 24
ROWQ = 16
PAIR = 2
TILES_PER_TRIP = 8


def _sort_network(n):
    def merge(lo, hi, r):
        step = 2 * r
        if step < hi - lo:
            yield from merge(lo, hi, step)
            yield from merge(lo + r, hi, step)
            yield from ((i, i + r) for i in range(lo + r, hi - r, step))
        else:
            yield (lo, lo + r)

    def sort(lo, hi):
        if hi - lo >= 1:
            mid = lo + (hi - lo) // 2
            yield from sort(lo, mid)
            yield from sort(mid + 1, hi)
            yield from merge(lo, hi, 1)

    return list(sort(0, n - 1))


SORT16 = _sort_network(16)


def _extract_top(vals, n):
    tiles = [vals[8 * i:8 * (i + 1), :] for i in range(16)]
    for i, j in SORT16:
        tiles[i], tiles[j] = jnp.maximum(tiles[i], tiles[j]), jnp.minimum(tiles[i], tiles[j])
    out = []
    for k in range(n):
        m = jnp.max(tiles[0], axis=0, keepdims=True)
        out.append(m)
        hit = tiles[0] == m
        below = tiles[1:] + [jnp.full_like(m, -jnp.inf)]
        keep = min(n - k - 1, len(tiles))
        tiles = [jnp.where(hit, below[i], tiles[i]) for i in range(keep)]
    return out


def _peer_kernel(h_ref, xn_ref, wqT_ref, keys_ref, u_ref, vT_ref, o_ref,
                 s_sc, vtop_sc, e2_sc, thr_sc, a1_sc, at0_sc, at1_sc, c0_sc, c1_sc, outT_sc,
                 *, tb, ec):
    j = pl.program_id(1)
    ncol = tb // LANES
    n_i1 = ec // PEER_NKEYS_
    n_chunk = PEER_NKEYS_ // n_i1

    @pl.when(j == 0)
    def _scores_and_thresholds():
        qT = lax.dot_general(wqT_ref[...], xn_ref[...], NT_DIMS,
                             preferred_element_type=F32).astype(BF16)
        for hc in range(2 * PEER_HEADS_):
            sT = _dot(keys_ref[hc], qT[hc * PEER_DK_HALF_:(hc + 1) * PEER_DK_HALF_, :])
            for c in range(ncol):
                s_sc[hc, c] = sT[:, c * LANES:(c + 1) * LANES]
        vtop_sc[...] = jnp.full(vtop_sc.shape, -jnp.inf, F32)

        def thr_tile(idx):
            h = idx // ncol
            col = idx % ncol
            for c in range(2):
                for r, m in enumerate(_extract_top(s_sc[2 * h + c, col], N_TOP)):
                    vtop_sc[2 * h + c, col, r:r + 1, :] = m
            v1 = vtop_sc[2 * h, col]
            v2 = vtop_sc[2 * h + 1, col]
            blocks = [v1[0:1] + v2]
            blocks += [v1[a:a + 1] + v2[0:8] for a in range(1, 8)]
            blocks += [v1[8:TOP_ROWS] + v2[0:1]]
            blocks += [jnp.full((PEER_NKEYS_ - 4 * TOP_ROWS, LANES), -jnp.inf, F32)]
            cs = _extract_top(jnp.concatenate(blocks, axis=0), N_TOP)
            tau = 0.5 * (cs[PEER_TOPK_ - 1] + cs[PEER_TOPK_])
            z = jnp.exp(cs[0] - cs[0])
            for r in range(1, PEER_TOPK_):
                z = z + jnp.exp(cs[r] - cs[0])
            s1 = s_sc[2 * h, col]
            s2 = s_sc[2 * h + 1, col]
            e2_sc[h, col] = jnp.exp(s2 - v2[0:1])
            thr_sc[:, col, h] = (tau - s1).reshape(n_chunk, n_i1, LANES)
            a1_sc[:, col, h] = (0.5 * jnp.exp(s1 - v1[0:1]) / z).reshape(n_chunk, n_i1, LANES)

        def thr_body(idx, carry):
            for sub in range(TILES_PER_TRIP):
                thr_tile(TILES_PER_TRIP * idx + sub)
            return carry

        lax.fori_loop(0, PEER_HEADS_ * ncol // TILES_PER_TRIP, thr_body, 0)
        outT_sc[...] = jnp.zeros(outT_sc.shape, F32)

    def zero_after(x):
        bits = lax.bitcast_convert_type(x, jnp.uint32)
        return lax.bitcast_convert_type((bits >> 16) >> 16, F32)

    def gate_column(at_src, c_dst, chunk, col, dep):
        for r0 in range(0, PEER_NKEYS_, ROWQ):
            zero = zero_after(dep)
            acc = [zero for _ in range(n_i1)]
            for h in range(PEER_HEADS_):
                s2 = s_sc[2 * h + 1, col, r0:r0 + ROWQ, :]
                e2 = e2_sc[h, col, r0:r0 + ROWQ, :]
                for i1l in range(n_i1):
                    thr = thr_sc[chunk, col, h, i1l:i1l + 1, :]
                    a1 = a1_sc[chunk, col, h, i1l:i1l + 1, :]
                    acc[i1l] = acc[i1l] + jnp.where(s2 >= thr, e2, 0.0) * a1
            for i1l in range(n_i1):
                rows = slice(i1l * PEER_NKEYS_ + r0, i1l * PEER_NKEYS_ + r0 + ROWQ)
                a = at_src[col, rows, :]
                dep = a * (1.0 + lax.erf(a * np.float32(math.sqrt(0.5)))) * acc[i1l]
                c_dst[rows, col * LANES:(col + 1) * LANES] = dep.astype(BF16)
        return dep

    hw = ec // 2
    dep = xn_ref[0:ROWQ, 0:LANES].astype(F32)
    for k, (at_sc, c_sc) in enumerate(((at0_sc, c0_sc), (at1_sc, c1_sc))):
        u_k = pltpu.bitcast(u_ref[k * hw:(k + 1) * hw, :], BF16)
        for n0 in range(0, tb, 2 * LANES):
            at = lax.dot_general(u_k, xn_ref[n0:n0 + 2 * LANES, :], NT_DIMS,
                                 preferred_element_type=F32)
            at_sc[n0 // LANES] = at[:, 0:LANES]
            at_sc[n0 // LANES + 1] = at[:, LANES:2 * LANES]
    for k, (at_sc, c_sc) in enumerate(((at0_sc, c0_sc), (at1_sc, c1_sc))):
        vt_k = pltpu.bitcast(vT_ref[:, k * ec:(k + 1) * ec], BF16)
        for n0 in range(0, tb, 2 * LANES):
            for col in (n0 // LANES, n0 // LANES + 1):
                dep = gate_column(at_sc, c_sc, PAIR * j + k, col, dep)
            outT_sc[:, n0:n0 + 2 * LANES] += _dot(vt_k, c_sc[:, n0:n0 + 2 * LANES])

    @pl.when(j == pl.num_programs(1) - 1)
    def _fin():
        o_ref[...] = h_ref[...] + outT_sc[...].T


def _peer(h1, hn, wqT, keys, u_tab, vT_tab, tb, ec):
    t = h1.shape[0]
    n_exp = 2 * u_tab.shape[0]
    ncol = tb // LANES
    n_i1 = ec // PEER_NKEYS_
    n_pairs = n_exp // (PAIR * ec)
    return pl.pallas_call(
        functools.partial(_peer_kernel, tb=tb, ec=ec),
        grid=(t // tb, n_pairs),
        in_specs=[
            pl.BlockSpec((tb, D_MODEL_), lambda i, j: (i, 0)),
            pl.BlockSpec((tb, D_MODEL_), lambda i, j: (i, 0)),
            pl.BlockSpec(wqT.shape, lambda i, j: (0, 0)),
            pl.BlockSpec(keys.shape, lambda i, j: (0, 0, 0)),
            pl.BlockSpec((PAIR * ec // 2, D_MODEL_), lambda i, j: (j, 0)),
            pl.BlockSpec((D_MODEL_ // 2, PAIR * ec), lambda i, j: (0, j)),
        ],
        out_specs=pl.BlockSpec((tb, D_MODEL_), lambda i, j: (i, 0)),
        out_shape=jax.ShapeDtypeStruct((t, D_MODEL_), F32),
        scratch_shapes=[
            pltpu.VMEM((2 * PEER_HEADS_, ncol, PEER_NKEYS_, LANES), F32),
            pltpu.VMEM((2 * PEER_HEADS_, ncol, TOP_ROWS, LANES), F32),
            pltpu.VMEM((PEER_HEADS_, ncol, PEER_NKEYS_, LANES), F32),
            pltpu.VMEM((n_exp // ec, ncol, PEER_HEADS_, n_i1, LANES), F32),
            pltpu.VMEM((n_exp // ec, ncol, PEER_HEADS_, n_i1, LANES), F32),
            pltpu.VMEM((ncol, ec, LANES), F32),
            pltpu.VMEM((ncol, ec, LANES), F32),
            pltpu.VMEM((ec, tb), BF16),
            pltpu.VMEM((ec, tb), BF16),
            pltpu.VMEM((D_MODEL_, tb), F32),
        ],
        compiler_params=pltpu.CompilerParams(
            dimension_semantics=("parallel", "arbitrary"), vmem_limit_bytes=VMEM_LIMIT),
        name="peer",
    )(h1, hn, wqT, keys, u_tab, vT_tab)


def _ple_kernel(h_ref, p_ref, g_pl_ref, wg_ref, bg_ref, wp_ref, g_fin_ref, o_ref, *, last_layer):
    h = h_ref[...]
    gate = jax.nn.sigmoid(_dot(_rms(h, g_pl_ref[...]).astype(BF16), wg_ref[...]) + bg_ref[...])
    h = h + gate * _dot(p_ref[...].astype(BF16), wp_ref[...])
    o_ref[...] = _rms(h, g_fin_ref[...]) if last_layer else h


def _ple(h2, p2, g_pl, wg, bg, wp, g_fin, last_layer, tm):
    t = h2.shape[0]
    full = lambda shape: pl.BlockSpec(shape, lambda i: (0,) * len(shape))
    return pl.pallas_call(
        functools.partial(_ple_kernel, last_layer=last_layer),
        grid=(t // tm,),
        in_specs=[
            pl.BlockSpec((tm, D_MODEL_), lambda i: (i, 0)),
            pl.BlockSpec((tm, PLE_DIM_), lambda i: (i, 0)),
            full((1, D_MODEL_)), full(wg.shape), full((1, D_MODEL_)), full(wp.shape), full((1, D_MODEL_)),
        ],
        out_specs=pl.BlockSpec((tm, D_MODEL_), lambda i: (i, 0)),
        out_shape=jax.ShapeDtypeStruct((t, D_MODEL_), F32),
        compiler_params=pltpu.CompilerParams(
            dimension_semantics=("parallel",), vmem_limit_bytes=VMEM_LIMIT),
        name="ple",
    )(h2, p2, g_pl, wg, bg, wp, g_fin)


def _rope_perm():
    pm = np.zeros((N_HEADS * QK_ROPE_, N_HEADS * HEAD_PAD), np.float32)
    for h in range(N_HEADS):
        for d in range(ROPE_HALF):
            pm[h * ROPE_HALF + d, h * HEAD_PAD + QK_NOPE_ + d] = 1.0
            pm[N_HEADS * ROPE_HALF + h * ROPE_HALF + d, h * HEAD_PAD + QK_NOPE_ + ROPE_HALF + d] = 1.0
    return jnp.asarray(pm, BF16)


def _pad_heads(w):
    return jnp.pad(w, ((0, 0), (0, 0), (0, HEAD_PAD - QK_NOPE_))).reshape(w.shape[0], -1)


def _prep_w_in(w_in):
    q_lat = w_in[:, 0:Q_RANK_]
    kv_lat = w_in[:, Q_RANK_:Q_RANK_ + KV_RANK_]
    o = Q_RANK_ + KV_RANK_
    k1 = jnp.tile(w_in[:, o:o + ROPE_HALF], (1, N_HEADS))
    k2 = jnp.tile(w_in[:, o + ROPE_HALF:o + QK_ROPE_], (1, N_HEADS))
    conv = w_in[:, o + QK_ROPE_:]
    return jnp.concatenate([q_lat, kv_lat, k1, k2, conv], axis=1).astype(BF16)


def _prep_w_uq(w_uq):
    w = w_uq.reshape(Q_RANK_, N_HEADS, QK_NOPE_ + QK_ROPE_)
    r1 = w[:, :, QK_NOPE_:QK_NOPE_ + ROPE_HALF].reshape(Q_RANK_, -1)
    r2 = w[:, :, QK_NOPE_ + ROPE_HALF:].reshape(Q_RANK_, -1)
    return _pad_heads(w[:, :, :QK_NOPE_]).astype(BF16), jnp.concatenate([r1, r2], axis=1).astype(BF16)


def _prep_w_ukv(w_ukv):
    w = w_ukv.reshape(KV_RANK_, N_HEADS, QK_NOPE_ + V_HEAD_)
    return _pad_heads(w[:, :, :QK_NOPE_]).astype(BF16), w[:, :, QK_NOPE_:].reshape(KV_RANK_, -1).astype(BF16)


def _pack_u_kernel(x_ref, o_ref):
    o_ref[...] = pltpu.bitcast(x_ref[...].astype(BF16), jnp.uint32)


def _pack_vt_kernel(x_ref, o_ref):
    o_ref[...] = pltpu.bitcast(x_ref[...].T.astype(BF16), jnp.uint32)


def _pack_tables(u_tab, v_tab, rows):
    n, d = u_tab.shape
    params = pltpu.CompilerParams(dimension_semantics=("parallel",), vmem_limit_bytes=VMEM_LIMIT)
    u_pk = pl.pallas_call(
        _pack_u_kernel, grid=(n // rows,),
        in_specs=[pl.BlockSpec((rows, d), lambda i: (i, 0))],
        out_specs=pl.BlockSpec((rows // 2, d), lambda i: (i, 0)),
        out_shape=jax.ShapeDtypeStruct((n // 2, d), jnp.uint32),
        compiler_params=params, name="pack_u")(u_tab)
    vt_pk = pl.pallas_call(
        _pack_vt_kernel, grid=(n // rows,),
        in_specs=[pl.BlockSpec((rows, d), lambda i: (i, 0))],
        out_specs=pl.BlockSpec((d // 2, rows), lambda i: (0, i)),
        out_shape=jax.ShapeDtypeStruct((d // 2, n), jnp.uint32),
        compiler_params=params, name="pack_vt")(v_tab)
    return u_pk, vt_pk


def _tile(n, want):
    return want if n % want == 0 else n


def kernel(x, p, positions, attn_norm, w_in, q_norm, w_uq, kv_norm, w_ukv, conv_w, conv_b, conv_ln_g, conv_ln_b, attn_out_norm, conv_out_norm, w_out, ffn_norm, peer_wq, peer_keys, peer_u, peer_v, pl_norm, pl_gate_w, pl_gate_b, pl_proj, final_norm):
    b, s, d = x.shape
    t = b * s
    depth = w_in.shape[0]
    tm = _tile(t, 512)
    tseq = _tile(s, 512)
    row = lambda a: a.reshape(1, -1)

    h = x.reshape(t, d)
    posf = positions.astype(F32).reshape(t, 1)
    half_freqs = ROPE_THETA_ ** (-jnp.arange(ROPE_HALF, dtype=F32) / ROPE_HALF)
    freq = jnp.tile(half_freqs, N_HEADS).reshape(1, LANES)
    perm = _rope_perm()

    for i in range(depth):
        q, k, v, glu = _inproj(h, posf, freq, row(attn_norm[i]), _prep_w_in(w_in[i]), row(q_norm[i]),
                               *_prep_w_uq(w_uq[i]), row(kv_norm[i]), *_prep_w_ukv(w_ukv[i]), perm, tm)
        att = _attention(q.reshape(b, s, -1), k.reshape(b, s, -1), v.reshape(b, s, -1), _tile(s, 1024))
        cnv = _conv_branch(glu.reshape(b, s, -1), conv_w[i], row(conv_b[i]), row(conv_ln_g[i]),
                           row(conv_ln_b[i]), row(conv_out_norm[i]), tseq)
        w_o = w_out[i].astype(BF16)
        h1, hn = _outproj(h, att.reshape(t, -1), cnv.reshape(t, -1), row(attn_out_norm[i]),
                          w_o[:N_HEADS * V_HEAD_], w_o[N_HEADS * V_HEAD_:], row(ffn_norm[i]), tm)
        keys = peer_keys[i].reshape(2 * PEER_HEADS_, PEER_NKEYS_, PEER_DK_HALF_).astype(BF16)
        u_pk, vt_pk = _pack_tables(peer_u[i], peer_v[i], 512)
        h2 = _peer(h1, hn, peer_wq[i].T.astype(BF16), keys, u_pk, vt_pk, tm, 1024)
        h = _ple(h2, p[i].reshape(t, -1), row(pl_norm[i]), pl_gate_w[i].astype(BF16),
                 row(pl_gate_b[i]), pl_proj[i].astype(BF16), row(final_norm), i == depth - 1, tm)
    return h.reshape(b, s, d)
```
